```python
import jax, jax.numpy as jnp
from jax import lax
import numpy as np

D_MODEL = 1024
BATCH = 4
SEQ = 4096
DEPTH = 2

HEAD_DIM = 64
BLOCK = 128
A_GROUPS = ((128, 1), (512, 4), (2048, 16))
A_HEADS = 12
A_W = A_HEADS * HEAD_DIM
B_Q_HEADS = 16
B_KV_HEADS = 2
B_WINDOW = 128
B_QW = B_Q_HEADS * HEAD_DIM
B_KVW = B_KV_HEADS * HEAD_DIM
C_HEADS = 8
C_KEY_DIM = 64
C_VAL_DIM = 128
C_CHUNK = 128
C_QK = C_HEADS * C_KEY_DIM
C_V = C_HEADS * C_VAL_DIM
N_BRANCH = 3
D_FF = 2816
ROPE_THETA = 10000.0
LN_EPS = 1e-5
ALPHA = (2.0 * DEPTH) ** 0.25
BETA = (8.0 * DEPTH) ** -0.25
IN_SPLITS = (A_W, A_W, A_W, A_W, A_W, A_W, A_W, A_W, A_W, B_QW, B_KVW, B_KVW, C_QK, C_QK, C_V, C_V, N_BRANCH * D_MODEL)
V_COLS = (2, 5, 8, 11, 14)
IN_COLS = sum(IN_SPLITS)

kernel_name = 'hybrid_dilated_swa_retention_macaron_deepnorm'


def layer_norm(x, gain, bias):
    xf = x.astype(jnp.float32)
    mu = jnp.mean(xf, axis=-1, keepdims=True)
    var = jnp.mean(jnp.square(xf - mu), axis=-1, keepdims=True)
    return ((xf - mu) * lax.rsqrt(var + LN_EPS) * gain.astype(jnp.float32) + bias.astype(jnp.float32)).astype(x.dtype)


def rope(x, positions):
    half = x.shape[-1] // 2
    inv = ROPE_THETA ** (-jnp.arange(half, dtype=jnp.float32) / half)
    ang = positions.astype(jnp.float32)[..., None] * inv
    cos = jnp.cos(ang)[:, :, None, :]
    sin = jnp.sin(ang)[:, :, None, :]
    xf = x.astype(jnp.float32)
    x1, x2 = xf[..., :half], xf[..., half:]
    return jnp.concatenate([x1 * cos - x2 * sin, x1 * sin + x2 * cos], axis=-1).astype(x.dtype)


def split_stride(x, d):
    b, s = x.shape[:2]
    rest = x.shape[2:]
    return x.reshape(b, s // d, d, *rest).swapaxes(1, 2).reshape(b * d, s // d, *rest)


def merge_stride(x, d, b):
    l = x.shape[1]
    rest = x.shape[2:]
    return x.reshape(b, d, l, *rest).swapaxes(1, 2).reshape(b, l * d, *rest)


def banded_attention(q, k, v, max_dist, sink=None):
    f32 = jnp.float32
    bsz, L, H, hd = q.shape
    G = k.shape[2]
    rep = H // G
    nb = -(-L // BLOCK)
    pad = nb * BLOCK - L
    padw = ((0, 0), (0, pad), (0, 0), (0, 0))
    qb = jnp.pad(q.astype(f32), padw).reshape(bsz, nb, BLOCK, G, rep, hd)
    kb = jnp.pad(k.astype(f32), padw).reshape(bsz, nb, BLOCK, G, hd)
    vb = jnp.pad(v.astype(f32), padw).reshape(bsz, nb, BLOCK, G, hd)

    def with_prev(t):
        prev = jnp.pad(t, ((0, 0), (1, 0), (0, 0), (0, 0), (0, 0)))[:, :-1]
        return jnp.concatenate([prev, t], axis=2)

    kk, vv = with_prev(kb), with_prev(vb)
    s = jnp.einsum('bnqgrd,bnkgd->bngrqk', qb, kk) * (hd ** -0.5)
    qi = jnp.arange(BLOCK)[:, None] + BLOCK
    ki = jnp.arange(2 * BLOCK)[None, :]
    dist = qi - ki
    band = (dist >= 0) & (dist <= max_dist)
    has_prev = (jnp.arange(nb) > 0)[:, None, None] | (ki >= BLOCK)[None]
    mask = band[None] & has_prev
    s = jnp.where(mask[None, :, None, None], s, -jnp.inf)
    m = jnp.max(s, axis=-1)
    if sink is not None:
        sk = sink.astype(f32).reshape(G, rep)[None, None, :, :, None]
        m = jnp.maximum(m, sk)
    e = jnp.exp(s - m[..., None])
    den = jnp.sum(e, axis=-1)
    if sink is not None:
        den = den + jnp.exp(sk - m)
    den_t = jnp.transpose(den, (0, 1, 4, 2, 3))
    o = jnp.einsum('bngrqk,bnkgd->bnqgrd', e, vv) / den_t[..., None]
    lse = jnp.transpose(m, (0, 1, 4, 2, 3)) + jnp.log(den_t)
    o = o.reshape(bsz, nb * BLOCK, H, hd)[:, :L]
    lse = lse.reshape(bsz, nb * BLOCK, H)[:, :L]
    return o, lse


def retention(q, k, v):
    f32 = jnp.float32
    bsz, S, H, dk = q.shape
    dv = v.shape[-1]
    n = S // C_CHUNK
    log_g = jnp.log1p(-jnp.exp2(-5.0 - jnp.arange(H, dtype=f32)))
    idx = jnp.arange(C_CHUNK, dtype=f32)
    rel = idx[:, None] - idx[None, :]
    intra = jnp.where(rel >= 0, jnp.exp(log_g[:, None, None] * jnp.maximum(rel, 0.0)), 0.0)
    q_dec = jnp.exp(log_g[:, None] * (idx + 1.0))[None, :, :, None]
    k_dec = jnp.exp(log_g[:, None] * (C_CHUNK - 1.0 - idx))[None, :, :, None]
    c_dec = jnp.exp(log_g * C_CHUNK)[None, :, None, None]

    def chunks(t):
        return t.astype(f32).reshape(bsz, n, C_CHUNK, H, t.shape[-1]).transpose(1, 0, 3, 2, 4)

    def step(state, inp):
        qi, ki, vi = inp
        a = jnp.einsum('bhqd,bhkd->bhqk', qi, ki) * intra
        o = jnp.einsum('bhqk,bhkv->bhqv', a, vi) + jnp.einsum('bhqd,bhdv->bhqv', qi, state) * q_dec
        state = state * c_dec + jnp.einsum('bhkd,bhkv->bhdv', ki * k_dec, vi)
        return state, o

    state0 = jnp.zeros((bsz, H, dk, dv), f32)
    _, o = lax.scan(step, state0, (chunks(q), chunks(k), chunks(v)))
    return o.transpose(1, 0, 3, 2, 4).reshape(bsz, S, H, dv)


def swiglu(x, w_up, w_down):
    a, b = jnp.split(x @ w_up, 2, axis=-1)
    return (jax.nn.silu(a) * b) @ w_down


def hybrid_mixer(x, positions, w_in, gate_bias, sinks, w_proj_a, w_proj_b, w_proj_c, w_out):
    bsz, S, _ = x.shape
    points, acc = [], 0
    for size in IN_SPLITS[:-1]:
        acc += size
        points.append(acc)
    parts = jnp.split(x @ w_in, points, axis=-1)

    def heads(t, nh):
        return t.reshape(bsz, S, nh, -1)

    outs, lses = [], []
    for gi, (window, dil) in enumerate(A_GROUPS):
        q = rope(heads(parts[3 * gi], A_HEADS), positions)
        k = rope(heads(parts[3 * gi + 1], A_HEADS), positions)
        v = heads(parts[3 * gi + 2], A_HEADS)
        o, lse = banded_attention(split_stride(q, dil), split_stride(k, dil), split_stride(v, dil), window // dil)
        outs.append(merge_stride(o, dil, bsz))
        lses.append(merge_stride(lse, dil, bsz))
    wts = jax.nn.softmax(jnp.stack(lses, axis=0), axis=0)[..., None]
    y_a = jnp.sum(wts * jnp.stack(outs, axis=0), axis=0).reshape(bsz, S, A_W).astype(x.dtype)

    qb = rope(heads(parts[9], B_Q_HEADS), positions)
    kb = rope(heads(parts[10], B_KV_HEADS), positions)
    vb = heads(parts[11], B_KV_HEADS)
    ob, _ = banded_attention(qb, kb, vb, B_WINDOW - 1, sinks)
    y_b = ob.reshape(bsz, S, B_QW).astype(x.dtype)

    qc = rope(heads(parts[12], C_HEADS), positions)
    kc = rope(heads(parts[13], C_HEADS), positions) * (C_KEY_DIM ** -0.5)
    vc = heads(parts[14], C_HEADS)
    r = retention(qc, kc, vc)
    mu = jnp.mean(r, axis=-1, keepdims=True)
    var = jnp.mean(jnp.square(r - mu), axis=-1, keepdims=True)
    r = ((r - mu) * lax.rsqrt(var + LN_EPS)).reshape(bsz, S, C_V)
    y_c = (jax.nn.silu(parts[15].astype(jnp.float32)) * r).astype(x.dtype)

    g_a, g_b, g_c = jnp.split(jax.nn.sigmoid(parts[16] + gate_bias), N_BRANCH, axis=-1)
    merged = g_a * (y_a @ w_proj_a) + g_b * (y_b @ w_proj_b) + g_c * (y_c @ w_proj_c)
    return merged @ w_out


def setup_inputs(seed: int = 0) -> dict:
    key = jax.random.key(seed)
    ks = jax.random.split(key, 20)
    f32 = jnp.float32

    def nrm(k, shape, scale):
        return jax.random.normal(k, shape, f32) * scale

    x = jax.random.normal(ks[0], (BATCH, SEQ, D_MODEL), f32)
    offsets = jax.random.randint(ks[1], (BATCH, 1), 0, 4096, dtype=jnp.int32)
    positions = (jnp.arange(SEQ, dtype=jnp.int32)[None, :] + offsets).astype(jnp.int32)
    col_scale = jnp.concatenate([jnp.full((n,), BETA if i in V_COLS else 1.0, f32) for i, n in enumerate(IN_SPLITS)])
    w_in = nrm(ks[2], (DEPTH, D_MODEL, IN_COLS), D_MODEL ** -0.5) * col_scale
    gate_bias = nrm(ks[3], (DEPTH, N_BRANCH * D_MODEL), 0.02)
    attn_sinks = nrm(ks[4], (DEPTH, B_Q_HEADS), 0.5)
    w_proj_a = nrm(ks[5], (DEPTH, A_W, D_MODEL), BETA * A_W ** -0.5)
    w_proj_b = nrm(ks[6], (DEPTH, B_QW, D_MODEL), BETA * B_QW ** -0.5)
    w_proj_c = nrm(ks[7], (DEPTH, C_V, D_MODEL), BETA * C_V ** -0.5)
    w_out = nrm(ks[8], (DEPTH, D_MODEL, D_MODEL), BETA * D_MODEL ** -0.5)
    ffn1_up = nrm(ks[9], (DEPTH, D_MODEL, 2 * D_FF), D_MODEL ** -0.5)
    ffn1_down = nrm(ks[10], (DEPTH, D_FF, D_MODEL), BETA * D_FF ** -0.5)
    ffn2_up = nrm(ks[11], (DEPTH, D_MODEL, 2 * D_FF), D_MODEL ** -0.5)
    ffn2_down = nrm(ks[12], (DEPTH, D_FF, D_MODEL), BETA * D_FF ** -0.5)
    ln1_g = 1.0 + nrm(ks[13], (DEPTH, D_MODEL), 0.02)
    ln1_b = nrm(ks[14], (DEPTH, D_MODEL), 0.02)
    ln2_g = 1.0 + nrm(ks[15], (DEPTH, D_MODEL), 0.02)
    ln2_b = nrm(ks[16], (DEPTH, D_MODEL), 0.02)
    ln3_g = 1.0 + nrm(ks[17], (DEPTH, D_MODEL), 0.02)
    ln3_b = nrm(ks[18], (DEPTH, D_MODEL), 0.02)
    return {'x': x, 'positions': positions, 'w_in': w_in, 'gate_bias': gate_bias, 'attn_sinks': attn_sinks,
            'w_proj_a': w_proj_a, 'w_proj_b': w_proj_b, 'w_proj_c': w_proj_c, 'w_out': w_out,
            'ffn1_up': ffn1_up, 'ffn1_down': ffn1_down, 'ffn2_up': ffn2_up, 'ffn2_down': ffn2_down,
            'ln1_g': ln1_g, 'ln1_b': ln1_b, 'ln2_g': ln2_g, 'ln2_b': ln2_b, 'ln3_g': ln3_g, 'ln3_b': ln3_b}


def reference(x, positions, w_in, gate_bias, attn_sinks, w_proj_a, w_proj_b, w_proj_c, w_out,
              ffn1_up, ffn1_down, ffn2_up, ffn2_down, ln1_g, ln1_b, ln2_g, ln2_b, ln3_g, ln3_b):
    for l in range(DEPTH):
        x = layer_norm(ALPHA * x + 0.5 * swiglu(x, ffn1_up[l], ffn1_down[l]), ln1_g[l], ln1_b[l])
        mix = hybrid_mixer(x, positions, w_in[l], gate_bias[l], attn_sinks[l],
                           w_proj_a[l], w_proj_b[l], w_proj_c[l], w_out[l])
        x = layer_norm(ALPHA * x + mix, ln2_g[l], ln2_b[l])
        x = layer_norm(ALPHA * x + 0.5 * swiglu(x, ffn2_up[l], ffn2_down[l]), ln3_g[l], ln3_b[l])
    return x
```

```python
import functools
import math

import jax
import jax.numpy as jnp
from jax import lax
from jax.experimental import pallas as pl
from jax.experimental.pallas import tpu as pltpu

D_MODEL = 1024
DEPTH = 2
HEAD_DIM = 64
BLOCK = 128
A_GROUPS = ((128, 1), (512, 4), (2048, 16))
A_HEADS = 12
A_W = A_HEADS * HEAD_DIM
B_Q_HEADS = 16
B_KV_HEADS = 2
B_WINDOW = 128
B_QW = B_Q_HEADS * HEAD_DIM
B_KVW = B_KV_HEADS * HEAD_DIM
C_HEADS = 8
C_KEY_DIM = 64
C_VAL_DIM = 128
C_CHUNK = 128
C_QK = C_HEADS * C_KEY_DIM
C_V = C_HEADS * C_VAL_DIM
D_FF = 2816
ROPE_THETA = 10000.0
LN_EPS = 1e-5
ALPHA = (2.0 * DEPTH) ** 0.25

LANES = 128
MXU_COLS = 256
FF_CHUNK = 256
NEG_BIG = -1e30
VMEM_LIMIT = 56 * 1024 * 1024

F32 = jnp.float32
BF16 = jnp.bfloat16


def _params(n_axes):
    return pltpu.CompilerParams(dimension_semantics=("arbitrary",) * n_axes,
                                vmem_limit_bytes=VMEM_LIMIT)


def _layer_norm(y, g, b):
    mu = jnp.mean(y, axis=-1, keepdims=True)
    yc = y - mu
    var = jnp.mean(yc * yc, axis=-1, keepdims=True)
    return yc * lax.rsqrt(var + LN_EPS) * g + b


def _full(shape):
    nd = len(shape)
    return pl.BlockSpec(shape, lambda *_: (0,) * nd)


def _rope_table_kernel(pos_ref, inv_ref, sign_ref, cos_ref, sin_ref):
    ang = pos_ref[...].astype(F32) * inv_ref[...]
    cos_ref[...] = jnp.cos(ang)
    sin_ref[...] = jnp.sin(ang) * sign_ref[...]


def rope_tables(pos_col, tm=1024):
    m = pos_col.shape[0]
    half = HEAD_DIM // 2
    inv = ROPE_THETA ** (-jnp.arange(half, dtype=F32) / half)
    inv_row = jnp.tile(inv, LANES // half)[None, :]
    lane = jnp.arange(LANES)
    sign_row = jnp.where(lane % HEAD_DIM < half, -1.0, 1.0).astype(F32)[None, :]
    return pl.pallas_call(
        _rope_table_kernel,
        grid=(m // tm,),
        in_specs=[pl.BlockSpec((tm, 1), lambda i: (i, 0)), _full((1, LANES)), _full((1, LANES))],
        out_specs=[pl.BlockSpec((tm, LANES), lambda i: (i, 0))] * 2,
        out_shape=[jax.ShapeDtypeStruct((m, LANES), F32)] * 2,
        compiler_params=_params(1),
        name="rope_table",
    )(pos_col, inv_row, sign_row)


def _ffn_kernel(x_ref, wa_ref, wb_ref, wd_ref, g_ref, b_ref, o_ref, acc_ref):
    x = x_ref[...]
    xb = x.astype(BF16)
    acc_ref[...] = jnp.zeros_like(acc_ref)

    def body(c, carry):
        a = jnp.dot(xb, wa_ref[c], preferred_element_type=F32)
        b = jnp.dot(xb, wb_ref[c], preferred_element_type=F32)
        h = (a * jax.nn.sigmoid(a) * b).astype(BF16)
        acc_ref[...] += jnp.dot(h, wd_ref[c], preferred_element_type=F32)
        return carry

    lax.fori_loop(0, wa_ref.shape[0], body, 0)
    y = ALPHA * x + 0.5 * acc_ref[...]
    o_ref[...] = _layer_norm(y, g_ref[...], b_ref[...])


def ffn_ln(x, wa, wb, wd, g, b, tm=512):
    m = x.shape[0]
    row = pl.BlockSpec((tm, D_MODEL), lambda i: (i, 0))
    return pl.pallas_call(
        _ffn_kernel,
        grid=(m // tm,),
        in_specs=[row, _full(wa.shape), _full(wb.shape), _full(wd.shape),
                  _full((1, D_MODEL)), _full((1, D_MODEL))],
        out_specs=row,
        out_shape=jax.ShapeDtypeStruct((m, D_MODEL), F32),
        scratch_shapes=[pltpu.VMEM((tm, D_MODEL), F32)],
        compiler_params=_params(1),
        name="ffn_ln",
    )(x, wa, wb, wd, g, b)


def _rope_slab(y, cos, sin_signed, first_half):
    partner = jnp.where(first_half, pltpu.roll(y, LANES - HEAD_DIM // 2, axis=1),
                        pltpu.roll(y, HEAD_DIM // 2, axis=1))
    return y * cos + partner * sin_signed


def _proj_kernel(x_ref, w_ref, cos_ref, sin_ref, *out_refs, segs):
    xb = x_ref[...].astype(BF16)
    cos = cos_ref[...]
    sin_signed = sin_ref[...]
    lane = lax.broadcasted_iota(jnp.int32, cos.shape, 1)
    first_half = (lane % HEAD_DIM) < (HEAD_DIM // 2)
    col = 0
    for o_ref, (width, rope, scale) in zip(out_refs, segs):
        for c0 in range(0, width, MXU_COLS):
            cw = min(MXU_COLS, width - c0)
            y2 = jnp.dot(xb, w_ref[:, col + c0:col + c0 + cw], preferred_element_type=F32)
            for s0 in range(0, cw, LANES):
                y = y2[:, s0:s0 + LANES]
                if rope:
                    y = _rope_slab(y, cos, sin_signed, first_half)
                if scale != 1.0:
                    y = y * scale
                o_ref[:, c0 + s0:c0 + s0 + LANES] = y.astype(o_ref.dtype)
        col += width


def proj_rope(x, w, cos, sin_signed, segs, name, tm=512):
    m = x.shape[0]
    n = w.shape[1]
    assert n == sum(s[0] for s in segs)
    return pl.pallas_call(
        functools.partial(_proj_kernel, segs=segs),
        grid=(m // tm,),
        in_specs=[pl.BlockSpec((tm, D_MODEL), lambda i: (i, 0)), _full(w.shape),
                  pl.BlockSpec((tm, LANES), lambda i: (i, 0)), pl.BlockSpec((tm, LANES), lambda i: (i, 0))],
        out_specs=[pl.BlockSpec((tm, s[0]), lambda i: (i, 0)) for s in segs],
        out_shape=[jax.ShapeDtypeStruct((m, s[0]), BF16) for s in segs],
        compiler_params=_params(1),
        name=name,
    )(x, w, cos, sin_signed)


def _attn_kernel(*refs, n_q_heads, n_kv_heads, max_dist, has_sink):
    if has_sink:
        sink_ref, q_ref, kc_ref, kp_ref, vc_ref, vp_ref, o_ref, lse_ref = refs
    else:
        q_ref, kc_ref, kp_ref, vc_ref, vp_ref, o_ref, lse_ref = refs
    n = pl.program_id(1)
    rep = n_q_heads // n_kv_heads
    qi = lax.broadcasted_iota(jnp.int32, (BLOCK, 2 * BLOCK), 0) + BLOCK
    ki = lax.broadcasted_iota(jnp.int32, (BLOCK, 2 * BLOCK), 1)
    dist = qi - ki
    mask = (dist >= 0) & (dist <= max_dist) & ((ki >= BLOCK) | (n > 0))
    lane = lax.broadcasted_iota(jnp.int32, (BLOCK, LANES), 1)
    lse_tile = jnp.zeros((BLOCK, LANES), F32)
    scale = HEAD_DIM ** -0.5
    for g in range(n_kv_heads):
        cs = slice(g * HEAD_DIM, (g + 1) * HEAD_DIM)
        k = jnp.concatenate([kp_ref[:, cs], kc_ref[:, cs]], axis=0)
        v = jnp.concatenate([vp_ref[:, cs], vc_ref[:, cs]], axis=0)
        for r in range(rep):
            h = g * rep + r
            hs = slice(h * HEAD_DIM, (h + 1) * HEAD_DIM)
            s = lax.dot_general(q_ref[:, hs], k, (((1,), (1,)), ((), ())),
                                preferred_element_type=F32) * scale
            s = jnp.where(mask, s, NEG_BIG)
            m = jnp.max(s, axis=-1, keepdims=True)
            if has_sink:
                sk = sink_ref[h]
                m = jnp.maximum(m, sk)
            e = jnp.exp(s - m)
            den = jnp.sum(e, axis=-1, keepdims=True)
            if has_sink:
                den = den + jnp.exp(sk - m)
            o = jnp.dot(e.astype(BF16), v, preferred_element_type=F32) / den
            o_ref[:, hs] = o.astype(o_ref.dtype)
            lse_tile = jnp.where(lane == h, m + jnp.log(den), lse_tile)
    lse_ref[...] = lse_tile


def banded_attention(q, k, v, max_dist, sink=None):
    nb, length, qw = q.shape
    kw = k.shape[-1]
    n_q_heads, n_kv_heads = qw // HEAD_DIM, kw // HEAD_DIM
    cur = lambda i, n: (i, n, 0)
    prev = lambda i, n: (i, jnp.maximum(n - 1, 0), 0)
    in_specs = [pl.BlockSpec((None, BLOCK, qw), cur),
                pl.BlockSpec((None, BLOCK, kw), cur), pl.BlockSpec((None, BLOCK, kw), prev),
                pl.BlockSpec((None, BLOCK, kw), cur), pl.BlockSpec((None, BLOCK, kw), prev)]
    args = [q, k, k, v, v]
    if sink is not None:
        in_specs = [pl.BlockSpec(memory_space=pltpu.SMEM)] + in_specs
        args = [sink] + args
    return pl.pallas_call(
        functools.partial(_attn_kernel, n_q_heads=n_q_heads, n_kv_heads=n_kv_heads,
                          max_dist=max_dist, has_sink=sink is not None),
        grid=(nb, length // BLOCK),
        in_specs=in_specs,
        out_specs=[pl.BlockSpec((None, BLOCK, qw), cur), pl.BlockSpec((None, BLOCK, LANES), cur)],
        out_shape=[jax.ShapeDtypeStruct((nb, length, qw), BF16),
                   jax.ShapeDtypeStruct((nb, length, LANES), F32)],
        compiler_params=_params(2),
        name="banded_attention",
    )(*args)


def _retention_kernel(q_ref, k_ref, v_ref, gate_ref, o_ref, state_ref):
    n = pl.program_id(1)

    @pl.when(n == 0)
    def _():
        state_ref[...] = jnp.zeros_like(state_ref)

    ii = lax.broadcasted_iota(jnp.int32, (C_CHUNK, C_CHUNK), 0)
    jj = lax.broadcasted_iota(jnp.int32, (C_CHUNK, C_CHUNK), 1)
    rel = (ii - jj).astype(F32)
    idx = lax.broadcasted_iota(jnp.int32, (C_CHUNK, 1), 0).astype(F32)
    for h in range(C_HEADS):
        log_g = math.log1p(-2.0 ** (-5.0 - h))
        intra = jnp.where(rel >= 0, jnp.exp(log_g * jnp.maximum(rel, 0.0)), 0.0)
        q_dec = jnp.exp(log_g * (idx + 1.0))
        k_dec = jnp.exp(log_g * (C_CHUNK - 1.0 - idx))
        c_dec = math.exp(log_g * C_CHUNK)
        ks = slice(h * C_KEY_DIM, (h + 1) * C_KEY_DIM)
        vs = slice(h * C_VAL_DIM, (h + 1) * C_VAL_DIM)
        q = q_ref[:, ks]
        k = k_ref[:, ks]
        v = v_ref[:, vs]
        a = lax.dot_general(q, k, (((1,), (1,)), ((), ())), preferred_element_type=F32) * intra
        st = state_ref[h]
        o = (jnp.dot(a.astype(BF16), v, preferred_element_type=F32)
             + jnp.dot(q, st.astype(BF16), preferred_element_type=F32) * q_dec)
        kd = (k.astype(F32) * k_dec).astype(BF16)
        state_ref[h] = st * c_dec + lax.dot_general(kd, v, (((0,), (0,)), ((), ())),
                                                    preferred_element_type=F32)
        mu = jnp.mean(o, axis=-1, keepdims=True)
        oc = o - mu
        var = jnp.mean(oc * oc, axis=-1, keepdims=True)
        r = oc * lax.rsqrt(var + LN_EPS)
        gate = gate_ref[:, vs].astype(F32)
        o_ref[:, vs] = (gate * jax.nn.sigmoid(gate) * r).astype(o_ref.dtype)


def retention_gated(q, k, v, gate):
    bsz, seq, _ = q.shape
    cur = lambda b, n: (b, n, 0)
    return pl.pallas_call(
        _retention_kernel,
        grid=(bsz, seq // C_CHUNK),
        in_specs=[pl.BlockSpec((None, C_CHUNK, C_QK), cur), pl.BlockSpec((None, C_CHUNK, C_QK), cur),
                  pl.BlockSpec((None, C_CHUNK, C_V), cur), pl.BlockSpec((None, C_CHUNK, C_V), cur)],
        out_specs=pl.BlockSpec((None, C_CHUNK, C_V), cur),
        out_shape=jax.ShapeDtypeStruct((bsz, seq, C_V), BF16),
        scratch_shapes=[pltpu.VMEM((C_HEADS, C_KEY_DIM, C_VAL_DIM), F32)],
        compiler_params=_params(2),
        name="retention",
    )(q, k, v, gate)


def _merge_kernel(x_ref, o1_ref, o2_ref, o3_ref, l1_ref, l2_ref, l3_ref, yb_ref, yc_ref,
                  wg_ref, gb_ref, pa_ref, pb_ref, pc_ref, wo_ref, e_ref, g_ref, b_ref, out_ref):
    x = x_ref[...]
    xb = x.astype(BF16)
    l1, l2, l3 = l1_ref[...], l2_ref[...], l3_ref[...]
    m = jnp.maximum(jnp.maximum(l1, l2), l3)
    e1, e2, e3 = jnp.exp(l1 - m), jnp.exp(l2 - m), jnp.exp(l3 - m)
    inv = 1.0 / (e1 + e2 + e3)
    expand = e_ref[...]

    def per_lane(w):
        hi = w.astype(BF16)
        lo = (w - hi.astype(F32)).astype(BF16)
        return (jnp.dot(hi, expand, preferred_element_type=F32)
                + jnp.dot(lo, expand, preferred_element_type=F32))

    ya = (per_lane(e1 * inv) * o1_ref[...].astype(F32)
          + per_lane(e2 * inv) * o2_ref[...].astype(F32)
          + per_lane(e3 * inv) * o3_ref[...].astype(F32))

    def gate(j):
        cols = slice(j * D_MODEL, (j + 1) * D_MODEL)
        return jax.nn.sigmoid(jnp.dot(xb, wg_ref[:, cols], preferred_element_type=F32) + gb_ref[:, cols])

    merged = gate(0) * jnp.dot(ya.astype(BF16), pa_ref[...], preferred_element_type=F32)
    merged += gate(1) * jnp.dot(yb_ref[...], pb_ref[...], preferred_element_type=F32)
    merged += gate(2) * jnp.dot(yc_ref[...], pc_ref[...], preferred_element_type=F32)
    mix = jnp.dot(merged.astype(BF16), wo_ref[...], preferred_element_type=F32)
    out_ref[...] = _layer_norm(ALPHA * x + mix, g_ref[...], b_ref[...])


def merge_ln(x, o1, o2, o3, l1, l2, l3, yb, yc, wg, gb, pa, pb, pc, wo, g, b, tm=256):
    m = x.shape[0]
    head = jnp.arange(A_W) // HEAD_DIM
    expand = (jnp.arange(LANES)[:, None] == head[None, :]).astype(BF16)
    row = lambda w: pl.BlockSpec((tm, w), lambda i: (i, 0))
    return pl.pallas_call(
        _merge_kernel,
        grid=(m // tm,),
        in_specs=[row(D_MODEL), row(A_W), row(A_W), row(A_W), row(LANES), row(LANES), row(LANES),
                  row(B_QW), row(C_V), _full(wg.shape), _full(gb.shape), _full(pa.shape),
                  _full(pb.shape), _full(pc.shape), _full(wo.shape), _full(expand.shape),
                  _full((1, D_MODEL)), _full((1, D_MODEL))],
        out_specs=row(D_MODEL),
        out_shape=jax.ShapeDtypeStruct((m, D_MODEL), F32),
        compiler_params=_params(1),
        name="merge_ln",
    )(x, o1, o2, o3, l1, l2, l3, yb, yc, wg, gb, pa, pb, pc, wo, expand, g, b)


def _split_stride(t, d):
    b, s, w = t.shape
    return t.reshape(b, s // d, d, w).swapaxes(1, 2).reshape(b * d, s // d, w)


def _merge_stride(t, d, b):
    _, l, w = t.shape
    return t.reshape(b, d, l, w).swapaxes(1, 2).reshape(b, l * d, w)


def _ffn_weights(w_up, w_down):
    n = D_FF // FF_CHUNK
    up = w_up.astype(BF16).reshape(D_MODEL, 2, n, FF_CHUNK)
    wa = up[:, 0].transpose(1, 0, 2)
    wb = up[:, 1].transpose(1, 0, 2)
    wd = w_down.astype(BF16).reshape(n, FF_CHUNK, D_MODEL)
    return wa, wb, wd


def kernel(x, positions, w_in, gate_bias, attn_sinks, w_proj_a, w_proj_b, w_proj_c, w_out, ffn1_up, ffn1_down, ffn2_up, ffn2_down, ln1_g, ln1_b, ln2_g, ln2_b, ln3_g, ln3_b):
    bsz, seq, _ = x.shape
    m = bsz * seq
    cos, sin_signed = rope_tables(positions.reshape(m, 1))
    x = x.reshape(m, D_MODEL)
    a_end = 9 * A_W
    b_end = a_end + B_QW + 2 * B_KVW
    c_end = b_end + 2 * C_QK + 2 * C_V
    seg_a = ((A_W, True, 1.0), (A_W, True, 1.0), (A_W, False, 1.0))
    seg_b = ((B_QW, True, 1.0), (B_KVW, True, 1.0), (B_KVW, False, 1.0))
    seg_c = ((C_QK, True, 1.0), (C_QK, True, C_KEY_DIM ** -0.5), (C_V, False, 1.0), (C_V, False, 1.0))
    row = lambda t: t.reshape(1, -1)
    for l in range(DEPTH):
        x = ffn_ln(x, *_ffn_weights(ffn1_up[l], ffn1_down[l]), row(ln1_g[l]), row(ln1_b[l]))
        w = w_in[l].astype(BF16)
        outs, lses = [], []
        for gi, (window, dil) in enumerate(A_GROUPS):
            q, k, v = proj_rope(x, w[:, 3 * gi * A_W:3 * (gi + 1) * A_W], cos, sin_signed, seg_a, "proj_a")
            q, k, v = (_split_stride(t.reshape(bsz, seq, A_W), dil) for t in (q, k, v))
            o, lse = banded_attention(q, k, v, window // dil)
            outs.append(_merge_stride(o, dil, bsz).reshape(m, A_W))
            lses.append(_merge_stride(lse, dil, bsz).reshape(m, LANES))
        qb, kb, vb = proj_rope(x, w[:, a_end:b_end], cos, sin_signed, seg_b, "proj_b")
        yb, _ = banded_attention(qb.reshape(bsz, seq, B_QW), kb.reshape(bsz, seq, B_KVW),
                                 vb.reshape(bsz, seq, B_KVW), B_WINDOW - 1, attn_sinks[l])
        qc, kc, vc, gc = proj_rope(x, w[:, b_end:c_end], cos, sin_signed, seg_c, "proj_c")
        yc = retention_gated(qc.reshape(bsz, seq, C_QK), kc.reshape(bsz, seq, C_QK),
                             vc.reshape(bsz, seq, C_V), gc.reshape(bsz, seq, C_V))
        x = merge_ln(x, *outs, *lses, yb.reshape(m, B_QW), yc.reshape(m, C_V),
                     w[:, c_end:], row(gate_bias[l]), w_proj_a[l].astype(BF16), w_proj_b[l].astype(BF16),
                     w_proj_c[l].astype(BF16), w_out[l].astype(BF16), row(ln2_g[l]), row(ln2_b[l]))
        x = ffn_ln(x, *_ffn_weights(ffn2_up[l], ffn2_down[l]), row(ln3_g[l]), row(ln3_b[l]))
    return x.reshape(bsz, seq, D_MODEL)
```

```python
import functools
import math

import jax
import jax.numpy as jnp
from jax import lax
from jax.experimental import pallas as pl
from jax.experimental.pallas import tpu as pltpu

D_MODEL = 1024
DEPTH = 2
HEAD_DIM = 64
BLOCK = 128
A_GROUPS = ((128, 1), (512, 4), (2048, 16))
A_HEADS = 12
A_W = A_HEADS * HEAD_DIM
B_Q_HEADS = 16
B_KV_HEADS = 2
B_WINDOW = 128
B_QW = B_Q_HEADS * HEAD_DIM
B_KVW = B_KV_HEADS * HEAD_DIM
C_HEADS = 8
C_KEY_DIM = 64
C_VAL_DIM = 128
C_CHUNK = 128
C_QK = C_HEADS * C_KEY_DIM
C_V = C_HEADS * C_VAL_DIM
D_FF = 2816
ROPE_THETA = 10000.0
LN_EPS = 1e-5
ALPHA = (2.0 * DEPTH) ** 0.25

LANES = 128
MXU_COLS = 256
FF_CHUNK = 256
SPAN = 2048
PLAIN_SPAN = 512
NEG_BIG = -1e30
VMEM_LIMIT = 56 * 1024 * 1024
N_SLABS = D_MODEL // LANES

F32 = jnp.float32
BF16 = jnp.bfloat16
NT = (((1,), (1,)), ((), ()))


def _params(n_axes):
    return pltpu.CompilerParams(dimension_semantics=("arbitrary",) * n_axes,
                                vmem_limit_bytes=VMEM_LIMIT)


def _layer_norm(y, g, b):
    mu = jnp.mean(y, axis=-1, keepdims=True)
    yc = y - mu
    var = jnp.mean(yc * yc, axis=-1, keepdims=True)
    return yc * lax.rsqrt(var + LN_EPS) * g + b


def _full(shape):
    nd = len(shape)
    return pl.BlockSpec(shape, lambda *_: (0,) * nd)


def _from_slabs(ref):
    return jnp.concatenate([ref[j] for j in range(ref.shape[0])], axis=1)


def _rope_table_kernel(pos_ref, inv_ref, sign_ref, cos_ref, sin_ref):
    ang = pos_ref[...].astype(F32) * inv_ref[...]
    cos_ref[...] = jnp.cos(ang)
    sin_ref[...] = jnp.sin(ang) * sign_ref[...]


def rope_tables(pos_col, tm=1024):
    m = pos_col.shape[0]
    half = HEAD_DIM // 2
    inv = ROPE_THETA ** (-jnp.arange(half, dtype=F32) / half)
    inv_row = jnp.tile(inv, LANES // half)[None, :]
    lane = jnp.arange(LANES)
    sign_row = jnp.where(lane % HEAD_DIM < half, -1.0, 1.0).astype(F32)[None, :]
    return pl.pallas_call(
        _rope_table_kernel,
        grid=(m // tm,),
        in_specs=[pl.BlockSpec((tm, 1), lambda i: (i, 0)), _full((1, LANES)), _full((1, LANES))],
        out_specs=[pl.BlockSpec((tm, LANES), lambda i: (i, 0))] * 2,
        out_shape=[jax.ShapeDtypeStruct((m, LANES), F32)] * 2,
        compiler_params=_params(1),
        name="rope_table",
    )(pos_col, inv_row, sign_row)


def _ffn_kernel(x_ref, wa_ref, wb_ref, wd_ref, g_ref, b_ref, *rest, slab_out):
    out_refs, acc_ref = rest[:-1], rest[-1]
    x = x_ref[...]
    xb = x.astype(BF16)
    acc_ref[...] = jnp.zeros_like(acc_ref)

    def body(c, carry):
        a = jnp.dot(xb, wa_ref[c], preferred_element_type=F32)
        b = jnp.dot(xb, wb_ref[c], preferred_element_type=F32)
        h = (a * jax.nn.sigmoid(a) * b).astype(BF16)
        acc_ref[...] += jnp.dot(h, wd_ref[c], preferred_element_type=F32)
        return carry

    lax.fori_loop(0, wa_ref.shape[0], body, 0)
    y = _layer_norm(ALPHA * x + 0.5 * acc_ref[...], g_ref[...], b_ref[...])
    if slab_out:
        slab_ref, bf_ref = out_refs
        for j in range(N_SLABS):
            slab_ref[j] = y[:, j * LANES:(j + 1) * LANES]
        bf_ref[...] = y.astype(BF16)
    else:
        out_refs[0][...] = y


def ffn_ln(x, wa, wb, wd, g, b, slab_out, tm=512):
    m = x.shape[0]
    row = pl.BlockSpec((tm, D_MODEL), lambda i: (i, 0))
    if slab_out:
        out_specs = [pl.BlockSpec((N_SLABS, tm, LANES), lambda i: (0, i, 0)), row]
        out_shape = [jax.ShapeDtypeStruct((N_SLABS, m, LANES), F32), jax.ShapeDtypeStruct((m, D_MODEL), BF16)]
    else:
        out_specs = row
        out_shape = jax.ShapeDtypeStruct((m, D_MODEL), F32)
    return pl.pallas_call(
        functools.partial(_ffn_kernel, slab_out=slab_out),
        grid=(m // tm,),
        in_specs=[row, _full(wa.shape), _full(wb.shape), _full(wd.shape),
                  _full((1, D_MODEL)), _full((1, D_MODEL))],
        out_specs=out_specs,
        out_shape=out_shape,
        scratch_shapes=[pltpu.VMEM((tm, D_MODEL), F32)],
        compiler_params=_params(1),
        name="ffn_ln",
    )(x, wa, wb, wd, g, b)


def _permute_kernel(x_ref, *out_refs, dils):
    for o_ref, d in zip(out_refs, dils):
        chunk = SPAN // d
        for r in range(d):
            for j in range(N_SLABS):
                o_ref[r * chunk:(r + 1) * chunk, j * LANES:(j + 1) * LANES] = (
                    x_ref[j, pl.ds(r, chunk, stride=d), :].astype(BF16))


def permute_tokens(x_slabs, dils):
    m = x_slabs.shape[1]
    return pl.pallas_call(
        functools.partial(_permute_kernel, dils=dils),
        grid=(m // SPAN,),
        in_specs=[pl.BlockSpec((N_SLABS, SPAN, LANES), lambda i: (0, i, 0))],
        out_specs=[pl.BlockSpec((SPAN, D_MODEL), lambda i: (i, 0)) for _ in dils],
        out_shape=[jax.ShapeDtypeStruct((m, D_MODEL), BF16) for _ in dils],
        compiler_params=_params(1),
        name="permute_tokens",
    )(x_slabs)


def _rope_slab(y, cos, sin_signed, first_half):
    partner = jnp.where(first_half, pltpu.roll(y, LANES - HEAD_DIM // 2, axis=1),
                        pltpu.roll(y, HEAD_DIM // 2, axis=1))
    return y * cos + partner * sin_signed


def _proj_kernel(x_ref, w_ref, wt_ref, cos_ref, sin_ref, *out_refs, segs, t_width):
    xb = x_ref[...]
    cos = cos_ref[...]
    sin_signed = sin_ref[...]
    lane = lax.broadcasted_iota(jnp.int32, cos.shape, 1)
    first_half = (lane % HEAD_DIM) < (HEAD_DIM // 2)
    col = 0
    for o_ref, (width, rope, scale) in zip(out_refs, segs):
        for c0 in range(0, width, MXU_COLS):
            cw = min(MXU_COLS, width - c0)
            y2 = jnp.dot(xb, w_ref[:, col + c0:col + c0 + cw], preferred_element_type=F32)
            for s0 in range(0, cw, LANES):
                y = y2[:, s0:s0 + LANES]
                if rope:
                    y = _rope_slab(y, cos, sin_signed, first_half)
                if scale != 1.0:
                    y = y * scale
                o_ref[:, c0 + s0:c0 + s0 + LANES] = y.astype(o_ref.dtype)
        col += width
    if t_width:
        t_ref = out_refs[len(segs)]
        for c0 in range(0, t_width, MXU_COLS):
            cw = min(MXU_COLS, t_width - c0)
            t_ref[c0:c0 + cw, :] = lax.dot_general(wt_ref[c0:c0 + cw, :], xb, NT,
                                                   preferred_element_type=F32).astype(t_ref.dtype)


def proj_rope(xb, w, wt, cos, sin_signed, segs, name, tm=512):
    m = xb.shape[0]
    assert w.shape[1] == sum(s[0] for s in segs)
    t_width = 0 if wt is None else wt.shape[0]
    if wt is None:
        wt = jnp.zeros((8, D_MODEL), BF16)
    row = lambda wd: pl.BlockSpec((tm, wd), lambda i: (i, 0))
    out_specs = [row(s[0]) for s in segs]
    out_shape = [jax.ShapeDtypeStruct((m, s[0]), BF16) for s in segs]
    if t_width:
        out_specs.append(pl.BlockSpec((t_width, tm), lambda i: (0, i)))
        out_shape.append(jax.ShapeDtypeStruct((t_width, m), BF16))
    return pl.pallas_call(
        functools.partial(_proj_kernel, segs=segs, t_width=t_width),
        grid=(m // tm,),
        in_specs=[row(D_MODEL), _full(w.shape), _full(wt.shape), row(LANES), row(LANES)],
        out_specs=out_specs,
        out_shape=out_shape,
        compiler_params=_params(1),
        name=name,
    )(xb, w, wt, cos, sin_signed)


def _attn_kernel(*refs, n_heads, gqa, max_dist, dil, n_blocks, spans_per_batch, has_sink, want_lse):
    refs = list(refs)
    sink_ref = refs.pop(0) if has_sink else None
    q_ref, kc_ref, kh_ref, vc_ref, vh_ref, o_ref = refs[:6]
    lse_ref = refs[6] if want_lse else None
    lse_scr = refs[-1]
    g = pl.program_id(0)
    r = pl.program_id(1)
    first_has_prev = (g % spans_per_batch) > 0
    key = lax.broadcasted_iota(jnp.int32, (BLOCK, 2 * BLOCK), 0)
    qry = lax.broadcasted_iota(jnp.int32, (BLOCK, 2 * BLOCK), 1) % BLOCK
    bias_cur = jnp.where(key <= qry, 0.0, NEG_BIG)
    in_band_prev = key >= qry + (BLOCK - max_dist)
    bias_prev_on = jnp.where(in_band_prev, 0.0, NEG_BIG)
    bias_prev_first = jnp.where(jnp.logical_and(in_band_prev, first_has_prev), 0.0, NEG_BIG)
    low = lax.broadcasted_iota(jnp.int32, (BLOCK, LANES), 1) < HEAD_DIM
    lane2 = lax.broadcasted_iota(jnp.int32, (1, 2 * BLOCK), 1)
    n_pairs = n_heads // 2
    rep = n_heads // (kc_ref.shape[1] // HEAD_DIM)
    units = [(i, p) for i in range(n_blocks) for p in range(n_pairs)]

    def rows_of(i):
        return slice(i * BLOCK, (i + 1) * BLOCK)

    def scores(u):
        i, p = units[u]
        rows = rows_of(i)
        qp = q_ref[rows, p * LANES:(p + 1) * LANES]
        zero = jnp.zeros_like(qp)
        if gqa:
            kv = (2 * p) // rep
            qr = pltpu.roll(qp, HEAD_DIM, axis=1)
            first, second = (qp, qr) if kv == 0 else (qr, qp)
            keep = low if kv == 0 else jnp.logical_not(low)
            rhs = jnp.concatenate([jnp.where(keep, first, zero), jnp.where(keep, second, zero)], axis=0)
            kcols = slice(0, LANES)
        else:
            rhs = jnp.concatenate([jnp.where(low, qp, zero), jnp.where(low, zero, qp)], axis=0)
            kcols = slice(p * LANES, (p + 1) * LANES)
        if i == 0:
            k_prev, bias_prev = kh_ref[:, kcols], bias_prev_first
        else:
            k_prev, bias_prev = kc_ref[rows_of(i - 1), kcols], bias_prev_on
        s_prev = lax.dot_general(k_prev, rhs, NT, preferred_element_type=F32) + bias_prev
        s_cur = lax.dot_general(kc_ref[rows, kcols], rhs, NT, preferred_element_type=F32) + bias_cur
        return s_prev, s_cur

    def softmax(u, s):
        _, p = units[u]
        s_prev, s_cur = s
        m = jnp.maximum(jnp.max(s_prev, axis=0, keepdims=True), jnp.max(s_cur, axis=0, keepdims=True))
        if has_sink:
            sink_row = jnp.where(lane2 < BLOCK, sink_ref[2 * p], sink_ref[2 * p + 1])
            m = jnp.maximum(m, sink_row)
        e_prev = jnp.exp(s_prev - m)
        e_cur = jnp.exp(s_cur - m)
        den = jnp.sum(e_prev, axis=0, keepdims=True) + jnp.sum(e_cur, axis=0, keepdims=True)
        if has_sink:
            den = den + jnp.exp(sink_row - m)
        return e_prev.astype(BF16), e_cur.astype(BF16), 1.0 / den, m + jnp.log(den)

    def values(u, sm):
        i, p = units[u]
        e_prev, e_cur = sm[:2]
        outs = []
        for hh in range(2):
            vh = ((2 * p) // rep) if gqa else (2 * p + hh)
            vrows = slice(vh * HEAD_DIM, (vh + 1) * HEAD_DIM)
            vt_prev = vh_ref[vrows, :] if i == 0 else vc_ref[vrows, rows_of(i - 1)]
            cols = slice(hh * BLOCK, (hh + 1) * BLOCK)
            outs.append(jnp.dot(vt_prev, e_prev[:, cols], preferred_element_type=F32)
                        + jnp.dot(vc_ref[vrows, rows_of(i)], e_cur[:, cols], preferred_element_type=F32))
        return outs

    def store(u, outs, sm):
        i, p = units[u]
        inv, lse = sm[2:]
        dst = rows_of(i) if dil == 1 else pl.ds(r + dil * BLOCK * i, BLOCK, stride=dil)
        o_ref[p, dst, :] = jnp.concatenate([outs[0] * inv[:, :BLOCK], outs[1] * inv[:, BLOCK:]], axis=0).T
        if want_lse:
            if p == 0:
                lse_scr[...] = jnp.zeros_like(lse_scr)
            lse_scr[2 * p:2 * p + 1, :] = lse[:, :BLOCK]
            lse_scr[2 * p + 1:2 * p + 2, :] = lse[:, BLOCK:]
            if p == n_pairs - 1:
                lse_ref[dst, :] = lse_scr[...].T

    n = len(units)
    s_val, sm_val, o_val = {}, {}, {}
    for t in range(n + 3):
        if t < n:
            s_val[t] = scores(t)
        if 0 <= t - 1 < n:
            sm_val[t - 1] = softmax(t - 1, s_val.pop(t - 1))
        if 0 <= t - 2 < n:
            o_val[t - 2] = values(t - 2, sm_val[t - 2])
        if 0 <= t - 3 < n:
            store(t - 3, o_val.pop(t - 3), sm_val.pop(t - 3))


def banded_attention(q, k, vt, max_dist, dil, seq, sink=None, want_lse=True):
    m, qw = q.shape
    kw = k.shape[1]
    n_heads = qw // HEAD_DIM
    gqa = kw != qw
    span = SPAN if dil > 1 else PLAIN_SPAN
    chunk = span // dil
    n_blocks = chunk // BLOCK
    spans_per_batch = seq // span

    def cur(g, r):
        return (g * dil + r, 0)

    def cur_t(g, r):
        return (0, g * dil + r)

    def halo_block(g, r):
        return jnp.maximum(((g - 1) * dil + r + 1) * n_blocks - 1, 0)

    in_specs = [pl.BlockSpec((chunk, qw), cur),
                pl.BlockSpec((chunk, kw), cur), pl.BlockSpec((BLOCK, kw), lambda g, r: (halo_block(g, r), 0)),
                pl.BlockSpec((kw, chunk), cur_t), pl.BlockSpec((kw, BLOCK), lambda g, r: (0, halo_block(g, r)))]
    args = [q, k, k, vt, vt]
    if sink is not None:
        in_specs = [pl.BlockSpec(memory_space=pltpu.SMEM)] + in_specs
        args = [sink] + args
    out_specs = [pl.BlockSpec((qw // LANES, span, LANES), lambda g, r: (0, g, 0))]
    out_shape = [jax.ShapeDtypeStruct((qw // LANES, m, LANES), F32)]
    if want_lse:
        out_specs.append(pl.BlockSpec((span, LANES), lambda g, r: (g, 0)))
        out_shape.append(jax.ShapeDtypeStruct((m, LANES), F32))
    return pl.pallas_call(
        functools.partial(_attn_kernel, n_heads=n_heads, gqa=gqa, max_dist=max_dist, dil=dil,
                          n_blocks=n_blocks, spans_per_batch=spans_per_batch,
                          has_sink=sink is not None, want_lse=want_lse),
        grid=(m // span, dil),
        in_specs=in_specs,
        out_specs=out_specs,
        out_shape=out_shape,
        scratch_shapes=[pltpu.VMEM((LANES, BLOCK), F32)],
        compiler_params=_params(2),
        name="banded_attention",
    )(*args)


def _retention_kernel(q_ref, k_ref, v_ref, gate_ref, o_ref, state_ref):
    n = pl.program_id(1)

    @pl.when(n == 0)
    def _():
        state_ref[...] = jnp.zeros_like(state_ref)

    ii = lax.broadcasted_iota(jnp.int32, (C_CHUNK, C_CHUNK), 0)
    jj = lax.broadcasted_iota(jnp.int32, (C_CHUNK, C_CHUNK), 1)
    rel = (ii - jj).astype(F32)
    idx = lax.broadcasted_iota(jnp.int32, (C_CHUNK, 1), 0).astype(F32)
    for h in range(C_HEADS):
        log_g = math.log1p(-2.0 ** (-5.0 - h))
        intra = jnp.where(rel >= 0, jnp.exp(log_g * jnp.maximum(rel, 0.0)), 0.0)
        q_dec = jnp.exp(log_g * (idx + 1.0))
        k_dec = jnp.exp(log_g * (C_CHUNK - 1.0 - idx))
        c_dec = math.exp(log_g * C_CHUNK)
        ks = slice(h * C_KEY_DIM, (h + 1) * C_KEY_DIM)
        vs = slice(h * C_VAL_DIM, (h + 1) * C_VAL_DIM)
        q = q_ref[:, ks]
        k = k_ref[:, ks]
        v = v_ref[:, vs]
        a = lax.dot_general(q, k, NT, preferred_element_type=F32) * intra
        st = state_ref[h]
        o = (jnp.dot(a.astype(BF16), v, preferred_element_type=F32)
             + jnp.dot(q, st.astype(BF16), preferred_element_type=F32) * q_dec)
        kd = (k.astype(F32) * k_dec).astype(BF16)
        state_ref[h] = st * c_dec + lax.dot_general(kd, v, (((0,), (0,)), ((), ())),
                                                    preferred_element_type=F32)
        mu = jnp.mean(o, axis=-1, keepdims=True)
        oc = o - mu
        var = jnp.mean(oc * oc, axis=-1, keepdims=True)
        r = oc * lax.rsqrt(var + LN_EPS)
        gate = gate_ref[:, vs].astype(F32)
        o_ref[:, vs] = (gate * jax.nn.sigmoid(gate) * r).astype(o_ref.dtype)


def retention_gated(q, k, v, gate):
    bsz, seq, _ = q.shape
    cur = lambda b, n: (b, n, 0)
    return pl.pallas_call(
        _retention_kernel,
        grid=(bsz, seq // C_CHUNK),
        in_specs=[pl.BlockSpec((None, C_CHUNK, C_QK), cur), pl.BlockSpec((None, C_CHUNK, C_QK), cur),
                  pl.BlockSpec((None, C_CHUNK, C_V), cur), pl.BlockSpec((None, C_CHUNK, C_V), cur)],
        out_specs=pl.BlockSpec((None, C_CHUNK, C_V), cur),
        out_shape=jax.ShapeDtypeStruct((bsz, seq, C_V), BF16),
        scratch_shapes=[pltpu.VMEM((C_HEADS, C_KEY_DIM, C_VAL_DIM), F32)],
        compiler_params=_params(2),
        name="retention",
    )(q, k, v, gate)


def _merge_kernel(x_ref, xb_ref, o1_ref, o2_ref, o3_ref, l1_ref, l2_ref, l3_ref, yb_ref, yc_ref,
                  wg_ref, gb_ref, pa_ref, pb_ref, pc_ref, wo_ref, e_ref, g_ref, b_ref, out_ref):
    x = _from_slabs(x_ref)
    xb = xb_ref[...]
    l1, l2, l3 = l1_ref[...], l2_ref[...], l3_ref[...]
    m = jnp.maximum(jnp.maximum(l1, l2), l3)
    e1, e2, e3 = jnp.exp(l1 - m), jnp.exp(l2 - m), jnp.exp(l3 - m)
    inv = 1.0 / (e1 + e2 + e3)
    expand = e_ref[...]

    def per_lane(w):
        hi = w.astype(BF16)
        lo = (w - hi.astype(F32)).astype(BF16)
        return (jnp.dot(hi, expand, preferred_element_type=F32)
                + jnp.dot(lo, expand, preferred_element_type=F32))

    ya = (per_lane(e1 * inv) * _from_slabs(o1_ref) + per_lane(e2 * inv) * _from_slabs(o2_ref)
          + per_lane(e3 * inv) * _from_slabs(o3_ref))

    def gate(j):
        cols = slice(j * D_MODEL, (j + 1) * D_MODEL)
        return jax.nn.sigmoid(jnp.dot(xb, wg_ref[:, cols], preferred_element_type=F32) + gb_ref[:, cols])

    merged = gate(0) * jnp.dot(ya.astype(BF16), pa_ref[...], preferred_element_type=F32)
    merged += gate(1) * jnp.dot(_from_slabs(yb_ref).astype(BF16), pb_ref[...], preferred_element_type=F32)
    merged += gate(2) * jnp.dot(yc_ref[...], pc_ref[...], preferred_element_type=F32)
    mix = jnp.dot(merged.astype(BF16), wo_ref[...], preferred_element_type=F32)
    out_ref[...] = _layer_norm(ALPHA * x + mix, g_ref[...], b_ref[...])


def merge_ln(x_slabs, xb, o1, o2, o3, l1, l2, l3, yb, yc, wg, gb, pa, pb, pc, wo, g, b, tm=256):
    m = xb.shape[0]
    head = jnp.arange(A_W) // HEAD_DIM
    expand = (jnp.arange(LANES)[:, None] == head[None, :]).astype(BF16)
    row = lambda w: pl.BlockSpec((tm, w), lambda i: (i, 0))
    slabs = lambda w: pl.BlockSpec((w // LANES, tm, LANES), lambda i: (0, i, 0))
    return pl.pallas_call(
        _merge_kernel,
        grid=(m // tm,),
        in_specs=[slabs(D_MODEL), row(D_MODEL), slabs(A_W), slabs(A_W), slabs(A_W),
                  row(LANES), row(LANES), row(LANES), slabs(B_QW), row(C_V),
                  _full(wg.shape), _full(gb.shape), _full(pa.shape),
                  _full(pb.shape), _full(pc.shape), _full(wo.shape), _full(expand.shape),
                  _full((1, D_MODEL)), _full((1, D_MODEL))],
        out_specs=row(D_MODEL),
        out_shape=jax.ShapeDtypeStruct((m, D_MODEL), F32),
        compiler_params=_params(1),
        name="merge_ln",
    )(x_slabs, xb, o1, o2, o3, l1, l2, l3, yb, yc, wg, gb, pa, pb, pc, wo, expand, g, b)


def _permuted_positions(positions, dil):
    bsz, seq = positions.shape
    span = SPAN if dil > 1 else PLAIN_SPAN
    p = positions.reshape(bsz, seq // span, span // dil, dil).swapaxes(2, 3)
    return p.reshape(bsz * seq, 1)


def _ffn_weights(w_up, w_down):
    n = D_FF // FF_CHUNK
    up = w_up.astype(BF16).reshape(D_MODEL, 2, n, FF_CHUNK)
    wa = up[:, 0].transpose(1, 0, 2)
    wb = up[:, 1].transpose(1, 0, 2)
    wd = w_down.astype(BF16).reshape(n, FF_CHUNK, D_MODEL)
    return wa, wb, wd


def kernel(x, positions, w_in, gate_bias, attn_sinks, w_proj_a, w_proj_b, w_proj_c, w_out, ffn1_up, ffn1_down, ffn2_up, ffn2_down, ln1_g, ln1_b, ln2_g, ln2_b, ln3_g, ln3_b):
    bsz, seq, _ = x.shape
    m = bsz * seq
    dils = tuple(d for _, d in A_GROUPS)
    tables = {d: rope_tables(_permuted_positions(positions, d)) for d in dils}
    x = x.reshape(m, D_MODEL)
    a_end = 9 * A_W
    b_end = a_end + B_QW + 2 * B_KVW
    c_end = b_end + 2 * C_QK + 2 * C_V
    qk_scale = HEAD_DIM ** -0.5
    seg_a = ((A_W, True, qk_scale), (A_W, True, 1.0))
    seg_b = ((B_QW, True, qk_scale), (B_KVW, True, 1.0))
    seg_c = ((C_QK, True, 1.0), (C_QK, True, C_KEY_DIM ** -0.5), (C_V, False, 1.0), (C_V, False, 1.0))
    row = lambda t: t.reshape(1, -1)
    for l in range(DEPTH):
        x_slabs, xb = ffn_ln(x, *_ffn_weights(ffn1_up[l], ffn1_down[l]), row(ln1_g[l]), row(ln1_b[l]), True)
        xb_by_dil = dict(zip(dils[1:], permute_tokens(x_slabs, dils[1:])))
        xb_by_dil[1] = xb
        w = w_in[l].astype(BF16)
        outs, lses = [], []
        for gi, (window, dil) in enumerate(A_GROUPS):
            c0 = 3 * gi * A_W
            q, k, vt = proj_rope(xb_by_dil[dil], w[:, c0:c0 + 2 * A_W], w[:, c0 + 2 * A_W:c0 + 3 * A_W].T,
                                 *tables[dil], seg_a, "proj_a")
            o, lse = banded_attention(q, k, vt, window // dil, dil, seq)
            outs.append(o)
            lses.append(lse)
        qb, kb, vtb = proj_rope(xb, w[:, a_end:a_end + B_QW + B_KVW], w[:, a_end + B_QW + B_KVW:b_end].T,
                                *tables[1], seg_b, "proj_b")
        yb, = banded_attention(qb, kb, vtb, B_WINDOW - 1, 1, seq, attn_sinks[l], want_lse=False)
        qc, kc, vc, gc = proj_rope(xb, w[:, b_end:c_end], None, *tables[1], seg_c, "proj_c")
        yc = retention_gated(qc.reshape(bsz, seq, C_QK), kc.reshape(bsz, seq, C_QK),
                             vc.reshape(bsz, seq, C_V), gc.reshape(bsz, seq, C_V))
        x = merge_ln(x_slabs, xb, *outs, *lses, yb, yc.reshape(m, C_V),
                     w[:, c_end:], row(gate_bias[l]), w_proj_a[l].astype(BF16), w_proj_b[l].astype(BF16),
                     w_proj_c[l].astype(BF16), w_out[l].astype(BF16), row(ln2_g[l]), row(ln2_b[l]))
        x = ffn_ln(x, *_ffn_weights(ffn2_up[l], ffn2_down[l]), row(ln3_g[l]), row(ln3_b[l]), False)
    return x.reshape(bsz, seq, D_MODEL)
```

```python
import functools
import math

import jax
import jax.numpy as jnp
from jax import lax
from jax.experimental import pallas as pl
from jax.experimental.pallas import tpu as pltpu

D_MODEL = 1024
DEPTH = 2
HEAD_DIM = 64
BLOCK = 128
A_GROUPS = ((128, 1), (512, 4), (2048, 16))
A_HEADS = 12
A_W = A_HEADS * HEAD_DIM
B_Q_HEADS = 16
B_KV_HEADS = 2
B_WINDOW = 128
B_QW = B_Q_HEADS * HEAD_DIM
B_KVW = B_KV_HEADS * HEAD_DIM
C_HEADS = 8
C_KEY_DIM = 64
C_VAL_DIM = 128
C_CHUNK = 128
C_QK = C_HEADS * C_KEY_DIM
C_V = C_HEADS * C_VAL_DIM
D_FF = 2816
ROPE_THETA = 10000.0
LN_EPS = 1e-5
ALPHA = (2.0 * DEPTH) ** 0.25

LANES = 128
MXU_COLS = 256
FF_CHUNK = 256
FFN_SUB = 512
SPAN = 2048
PLAIN_SPAN = 512
ATTN_UNITS_BLOCKS = 4
ATTN_LAGS = (3, 4, 7)
RET_LAGS = (1, 2, 3)
NEG_BIG = -1e30
ONES_ROWS = 16
LOG2_E = math.log2(math.e)
LN_2 = math.log(2.0)
VMEM_LIMIT = 56 * 1024 * 1024
N_SLABS = D_MODEL // LANES

F32 = jnp.float32
BF16 = jnp.bfloat16
NT = (((1,), (1,)), ((), ()))


def _params(n_axes):
    return pltpu.CompilerParams(dimension_semantics=("arbitrary",) * n_axes,
                                vmem_limit_bytes=VMEM_LIMIT)


def _layer_norm(y, g, b):
    mu = jnp.mean(y, axis=-1, keepdims=True)
    yc = y - mu
    var = jnp.mean(yc * yc, axis=-1, keepdims=True)
    return yc * lax.rsqrt(var + LN_EPS) * g + b


def _full(shape):
    nd = len(shape)
    return pl.BlockSpec(shape, lambda *_: (0,) * nd, pipeline_mode=pl.Buffered(1))


def _from_slabs(ref):
    return jnp.concatenate([ref[j] for j in range(ref.shape[0])], axis=1)


def _software_pipeline(n, stage_a, stage_b, stage_c, stage_d, lags=(1, 2, 3)):
    a_val, b_val, c_val = {}, {}, {}
    lag_b, lag_c, lag_d = lags
    for t in range(n + lag_d):
        if t < n:
            a_val[t] = stage_a(t)
        if 0 <= t - lag_b < n:
            b_val[t - lag_b] = stage_b(t - lag_b, a_val.pop(t - lag_b))
        if 0 <= t - lag_c < n:
            c_val[t - lag_c] = stage_c(t - lag_c, b_val[t - lag_c])
        if 0 <= t - lag_d < n:
            stage_d(t - lag_d, c_val.pop(t - lag_d), b_val.pop(t - lag_d))


def _rope_table_kernel(pos_ref, inv_ref, sign_ref, cos_ref, sin_ref):
    ang = pos_ref[...].astype(F32) * inv_ref[...]
    cos_ref[...] = jnp.cos(ang)
    sin_ref[...] = jnp.sin(ang) * sign_ref[...]


def rope_tables(pos_col, tm=1024):
    m = pos_col.shape[0]
    half = HEAD_DIM // 2
    inv = ROPE_THETA ** (-jnp.arange(half, dtype=F32) / half)
    inv_row = jnp.tile(inv, LANES // half)[None, :]
    lane = jnp.arange(LANES)
    sign_row = jnp.where(lane % HEAD_DIM < half, -1.0, 1.0).astype(F32)[None, :]
    return pl.pallas_call(
        _rope_table_kernel,
        grid=(m // tm,),
        in_specs=[pl.BlockSpec((tm, 1), lambda i: (i, 0)), _full((1, LANES)), _full((1, LANES))],
        out_specs=[pl.BlockSpec((tm, LANES), lambda i: (i, 0))] * 2,
        out_shape=[jax.ShapeDtypeStruct((m, LANES), F32)] * 2,
        compiler_params=_params(1),
        name="rope_table",
    )(pos_col, inv_row, sign_row)


def _ffn_kernel(x_ref, wu_ref, wd_ref, g_ref, b_ref, *out_refs, slab_out):
    n_chunks = D_FF // FF_CHUNK
    for sub in range(x_ref.shape[0] // FFN_SUB):
        rows = slice(sub * FFN_SUB, (sub + 1) * FFN_SUB)
        x = x_ref[rows, :]
        xb = x.astype(BF16)

        def hidden(c, xb=xb):
            lo = c * FF_CHUNK
            a = jnp.dot(xb, wu_ref[:, lo:lo + FF_CHUNK], preferred_element_type=F32)
            b = jnp.dot(xb, wu_ref[:, D_FF + lo:D_FF + lo + FF_CHUNK], preferred_element_type=F32)
            return (a * jax.nn.sigmoid(a) * b).astype(BF16)

        h = hidden(0)
        acc = None
        for c in range(n_chunks):
            h_next = hidden(c + 1) if c + 1 < n_chunks else None
            d = jnp.dot(h, wd_ref[c * FF_CHUNK:(c + 1) * FF_CHUNK, :], preferred_element_type=F32)
            acc = d if acc is None else acc + d
            h = h_next
        y = _layer_norm(ALPHA * x + 0.5 * acc, g_ref[...], b_ref[...])
        if slab_out:
            slab_ref, bf_ref = out_refs
            for j in range(N_SLABS):
                slab_ref[j, rows, :] = y[:, j * LANES:(j + 1) * LANES]
            bf_ref[rows, :] = y.astype(BF16)
        else:
            out_refs[0][rows, :] = y


def ffn_ln(x, wu, wd, g, b, slab_out, tm=1024):
    m = x.shape[0]
    row = pl.BlockSpec((tm, D_MODEL), lambda i: (i, 0))
    if slab_out:
        out_specs = [pl.BlockSpec((N_SLABS, tm, LANES), lambda i: (0, i, 0)), row]
        out_shape = [jax.ShapeDtypeStruct((N_SLABS, m, LANES), F32), jax.ShapeDtypeStruct((m, D_MODEL), BF16)]
    else:
        out_specs = row
        out_shape = jax.ShapeDtypeStruct((m, D_MODEL), F32)
    return pl.pallas_call(
        functools.partial(_ffn_kernel, slab_out=slab_out),
        grid=(m // tm,),
        in_specs=[row, _full(wu.shape), _full(wd.shape), _full((1, D_MODEL)), _full((1, D_MODEL))],
        out_specs=out_specs,
        out_shape=out_shape,
        compiler_params=_params(1),
        name="ffn_ln",
    )(x, wu, wd, g, b)


def _permute_kernel(x_ref, *out_refs, dils):
    for o_ref, d in zip(out_refs, dils):
        chunk = SPAN // d
        for r in range(d):
            for j in range(N_SLABS):
                o_ref[r * chunk:(r + 1) * chunk, j * LANES:(j + 1) * LANES] = (
                    x_ref[j, pl.ds(r, chunk, stride=d), :].astype(BF16))


def permute_tokens(x_slabs, dils):
    m = x_slabs.shape[1]
    return pl.pallas_call(
        functools.partial(_permute_kernel, dils=dils),
        grid=(m // SPAN,),
        in_specs=[pl.BlockSpec((N_SLABS, SPAN, LANES), lambda i: (0, i, 0))],
        out_specs=[pl.BlockSpec((SPAN, D_MODEL), lambda i: (i, 0)) for _ in dils],
        out_shape=[jax.ShapeDtypeStruct((m, D_MODEL), BF16) for _ in dils],
        compiler_params=_params(1),
        name="permute_tokens",
    )(x_slabs)


def _permute_table_kernel(cos_ref, sin_ref, *out_refs, dils):
    outs = iter(out_refs)
    for d in dils:
        chunk = SPAN // d
        for t_ref in (cos_ref, sin_ref):
            o_ref = next(outs)
            for r in range(d):
                o_ref[r * chunk:(r + 1) * chunk, :] = t_ref[pl.ds(r, chunk, stride=d), :]


def permute_tables(cos, sin_signed, dils):
    m = cos.shape[0]
    blk = pl.BlockSpec((SPAN, LANES), lambda i: (i, 0))
    outs = pl.pallas_call(
        functools.partial(_permute_table_kernel, dils=dils),
        grid=(m // SPAN,),
        in_specs=[blk, blk],
        out_specs=[blk] * (2 * len(dils)),
        out_shape=[jax.ShapeDtypeStruct((m, LANES), F32)] * (2 * len(dils)),
        compiler_params=_params(1),
        name="permute_tables",
    )(cos, sin_signed)
    return [(outs[2 * i], outs[2 * i + 1]) for i in range(len(dils))]


def _rope_slab(y, cos, sin_signed, first_half):
    partner = jnp.where(first_half, pltpu.roll(y, LANES - HEAD_DIM // 2, axis=1),
                        pltpu.roll(y, HEAD_DIM // 2, axis=1))
    return y * cos + partner * sin_signed


def _proj_kernel(x_ref, w_ref, wt_ref, cos_ref, sin_ref, *out_refs, segs, t_width):
    xb = x_ref[...]
    cos = cos_ref[...]
    sin_signed = sin_ref[...]
    lane = lax.broadcasted_iota(jnp.int32, cos.shape, 1)
    first_half = (lane % HEAD_DIM) < (HEAD_DIM // 2)
    col = 0
    for o_ref, (width, rope, scale) in zip(out_refs, segs):
        for c0 in range(0, width, MXU_COLS):
            cw = min(MXU_COLS, width - c0)
            y2 = jnp.dot(xb, w_ref[:, col + c0:col + c0 + cw], preferred_element_type=F32)
            for s0 in range(0, cw, LANES):
                y = y2[:, s0:s0 + LANES]
                if rope:
                    y = _rope_slab(y, cos, sin_signed, first_half)
                if scale != 1.0:
                    y = y * scale
                o_ref[:, c0 + s0:c0 + s0 + LANES] = y.astype(o_ref.dtype)
        col += width
    if t_width:
        t_ref = out_refs[len(segs)]
        for c0 in range(0, t_width, MXU_COLS):
            cw = min(MXU_COLS, t_width - c0)
            t_ref[c0:c0 + cw, :] = lax.dot_general(wt_ref[c0:c0 + cw, :], xb, NT,
                                                   preferred_element_type=F32).astype(t_ref.dtype)


def proj_rope(xb, w, wt, cos, sin_signed, segs, name, tm=512):
    m = xb.shape[0]
    assert w.shape[1] == sum(s[0] for s in segs)
    t_width = 0 if wt is None else wt.shape[0]
    if wt is None:
        wt = jnp.zeros((8, D_MODEL), BF16)
    row = lambda wd: pl.BlockSpec((tm, wd), lambda i: (i, 0))
    out_specs = [row(s[0]) for s in segs]
    out_shape = [jax.ShapeDtypeStruct((m, s[0]), BF16) for s in segs]
    if t_width:
        out_specs.append(pl.BlockSpec((t_width, tm), lambda i: (0, i)))
        out_shape.append(jax.ShapeDtypeStruct((t_width, m), BF16))
    return pl.pallas_call(
        functools.partial(_proj_kernel, segs=segs, t_width=t_width),
        grid=(m // tm,),
        in_specs=[row(D_MODEL), _full(w.shape), _full(wt.shape), row(LANES), row(LANES)],
        out_specs=out_specs,
        out_shape=out_shape,
        compiler_params=_params(1),
        name=name,
    )(xb, w, wt, cos, sin_signed)


def _attn_kernel(*refs, n_heads, gqa, max_dist, dil, n_res, n_blocks, spans_per_batch, has_sink, want_lse):
    refs = list(refs)
    sink_ref = refs.pop(0) if has_sink else None
    q_ref, kc_ref, kh_ref, vc_ref, vh_ref, o_ref = refs[:6]
    lse_ref = refs[6] if want_lse else None
    lse_scr = refs[-1]
    g = pl.program_id(0)
    r = pl.program_id(1)
    first_has_prev = (g % spans_per_batch) > 0
    key = lax.broadcasted_iota(jnp.int32, (2 * BLOCK, 2 * BLOCK), 0)
    qry = lax.broadcasted_iota(jnp.int32, (2 * BLOCK, 2 * BLOCK), 1) % BLOCK
    in_band_prev = jnp.logical_and(key < BLOCK, key >= qry + (BLOCK - max_dist))
    in_band_cur = jnp.logical_and(key >= BLOCK, key - BLOCK <= qry)
    bias_on = jnp.where(jnp.logical_or(in_band_prev, in_band_cur), 0.0, NEG_BIG)
    bias_first = jnp.where(jnp.logical_or(jnp.logical_and(in_band_prev, first_has_prev), in_band_cur),
                           0.0, NEG_BIG)
    low = lax.broadcasted_iota(jnp.int32, (BLOCK, LANES), 1) < HEAD_DIM
    lane2 = lax.broadcasted_iota(jnp.int32, (1, 2 * BLOCK), 1)
    ones_rows = jnp.ones((ONES_ROWS, 2 * BLOCK), BF16)
    n_pairs = n_heads // 2
    rep = n_heads // (kc_ref.shape[1] // HEAD_DIM)
    units = [(j, i, p) for j in range(n_res) for i in range(n_blocks) for p in range(n_pairs)]

    def rows_of(j, i):
        return slice((j * n_blocks + i) * BLOCK, (j * n_blocks + i + 1) * BLOCK)

    def prev_and_cur(j, i):
        return slice((j * n_blocks + i - 1) * BLOCK, (j * n_blocks + i + 1) * BLOCK)

    def scores(u):
        j, i, p = units[u]
        rows = rows_of(j, i)
        qp = q_ref[rows, p * LANES:(p + 1) * LANES]
        zero = jnp.zeros_like(qp)
        if gqa:
            kv = (2 * p) // rep
            qr = pltpu.roll(qp, HEAD_DIM, axis=1)
            first, second = (qp, qr) if kv == 0 else (qr, qp)
            keep = low if kv == 0 else jnp.logical_not(low)
            rhs = jnp.concatenate([jnp.where(keep, first, zero), jnp.where(keep, second, zero)], axis=0)
            kcols = slice(0, LANES)
        else:
            rhs = jnp.concatenate([jnp.where(low, qp, zero), jnp.where(low, zero, qp)], axis=0)
            kcols = slice(p * LANES, (p + 1) * LANES)
        if i == 0:
            keys = jnp.concatenate([kh_ref[j * BLOCK:(j + 1) * BLOCK, kcols], kc_ref[rows, kcols]], axis=0)
            bias = bias_first
        else:
            keys, bias = kc_ref[prev_and_cur(j, i), kcols], bias_on
        return lax.dot_general(keys, rhs, NT, preferred_element_type=F32) + bias

    def softmax(u, s):
        p = units[u][2]
        m = jnp.max(s, axis=0, keepdims=True)
        sink_row = None
        if has_sink:
            sink_row = jnp.where(lane2 < BLOCK, sink_ref[2 * p], sink_ref[2 * p + 1]) * LOG2_E
            m = jnp.maximum(m, sink_row)
        return jnp.exp2(s - m).astype(BF16), m, sink_row

    def values(u, sm):
        j, i, p = units[u]
        vrows = (slice(((2 * p) // rep) * HEAD_DIM, ((2 * p) // rep + 1) * HEAD_DIM) if gqa
                 else slice(p * LANES, (p + 1) * LANES))
        if i == 0:
            vt = jnp.concatenate([vh_ref[vrows, j * BLOCK:(j + 1) * BLOCK], vc_ref[vrows, rows_of(j, i)]], axis=1)
        else:
            vt = vc_ref[vrows, prev_and_cur(j, i)]
        return jnp.dot(jnp.concatenate([vt, ones_rows], axis=0), sm[0], preferred_element_type=F32)

    def store(u, res, sm):
        j, i, p = units[u]
        _, m, sink_row = sm
        n_feat = res.shape[0] - ONES_ROWS
        den = res[n_feat:n_feat + 1, :]
        if has_sink:
            den = den + jnp.exp2(sink_row - m)
        inv = 1.0 / den
        second = slice(0, HEAD_DIM) if gqa else slice(HEAD_DIM, 2 * HEAD_DIM)
        o_pair = jnp.concatenate([res[:HEAD_DIM, :BLOCK] * inv[:, :BLOCK],
                                  res[second, BLOCK:] * inv[:, BLOCK:]], axis=0)
        dst = rows_of(j, i) if dil == 1 else pl.ds(r * n_res + j + dil * BLOCK * i, BLOCK, stride=dil)
        o_ref[p, dst, :] = o_pair.T
        if want_lse:
            lse = (m + jnp.log2(den)) * LN_2
            if p == 0:
                lse_scr[...] = jnp.zeros_like(lse_scr)
            lse_scr[2 * p:2 * p + 1, :] = lse[:, :BLOCK]
            lse_scr[2 * p + 1:2 * p + 2, :] = lse[:, BLOCK:]
            if p == n_pairs - 1:
                lse_ref[dst, :] = lse_scr[...].T

    _software_pipeline(len(units), scores, softmax, values, store, lags=ATTN_LAGS)


def banded_attention(q, k, vt, max_dist, dil, seq, sink=None, want_lse=True):
    m, qw = q.shape
    kw = k.shape[1]
    n_heads = qw // HEAD_DIM
    gqa = kw != qw
    span = SPAN if dil > 1 else PLAIN_SPAN
    chunk = span // dil
    n_blocks = chunk // BLOCK
    n_res = ATTN_UNITS_BLOCKS // n_blocks
    assert n_res == 1 or n_blocks == 1
    spans_per_batch = seq // span
    steps = dil // n_res
    rows = chunk * n_res
    halo = BLOCK * n_res

    def cur(g, r):
        return (g * steps + r, 0)

    def cur_t(g, r):
        return (0, g * steps + r)

    def halo_block(g, r):
        return jnp.maximum((((g - 1) * dil + (r + 1) * n_res) * chunk - halo) // halo, 0)

    in_specs = [pl.BlockSpec((rows, qw), cur),
                pl.BlockSpec((rows, kw), cur), pl.BlockSpec((halo, kw), lambda g, r: (halo_block(g, r), 0)),
                pl.BlockSpec((kw, rows), cur_t), pl.BlockSpec((kw, halo), lambda g, r: (0, halo_block(g, r)))]
    args = [q, k, k, vt, vt]
    if sink is not None:
        in_specs = [pl.BlockSpec(memory_space=pltpu.SMEM)] + in_specs
        args = [sink] + args
    out_specs = [pl.BlockSpec((qw // LANES, span, LANES), lambda g, r: (0, g, 0))]
    out_shape = [jax.ShapeDtypeStruct((qw // LANES, m, LANES), F32)]
    if want_lse:
        out_specs.append(pl.BlockSpec((span, LANES), lambda g, r: (g, 0)))
        out_shape.append(jax.ShapeDtypeStruct((m, LANES), F32))
    return pl.pallas_call(
        functools.partial(_attn_kernel, n_heads=n_heads, gqa=gqa, max_dist=max_dist, dil=dil, n_res=n_res,
                          n_blocks=n_blocks, spans_per_batch=spans_per_batch,
                          has_sink=sink is not None, want_lse=want_lse),
        grid=(m // span, steps),
        in_specs=in_specs,
        out_specs=out_specs,
        out_shape=out_shape,
        scratch_shapes=[pltpu.VMEM((LANES, BLOCK), F32)],
        compiler_params=_params(2),
        name="banded_attention",
    )(*args)


def _retention_kernel(q_ref, k_ref, v_ref, gate_ref, o_ref, state_ref, dec_ref):
    n = pl.program_id(0)
    n_pairs = C_HEADS // 2
    log_g = [math.log1p(-2.0 ** (-5.0 - h)) for h in range(C_HEADS)]
    row = lax.broadcasted_iota(jnp.int32, (C_CHUNK, LANES), 0)
    lane = lax.broadcasted_iota(jnp.int32, (C_CHUNK, LANES), 1)
    low = lane < C_KEY_DIM

    @pl.when(n == 0)
    def _():
        state_ref[...] = jnp.zeros_like(state_ref)
        rel = (row - lane).astype(F32)
        idx = row.astype(F32)
        for h in range(C_HEADS):
            dec_ref[0, h] = jnp.where(rel >= 0, jnp.exp(log_g[h] * jnp.maximum(rel, 0.0)), 0.0)
            dec_ref[1, h] = jnp.exp(log_g[h] * (idx + 1.0))
        for p in range(n_pairs):
            lg = jnp.where(low, log_g[2 * p], log_g[2 * p + 1])
            dec_ref[2, p] = jnp.exp(lg * (C_CHUNK - 1.0 - idx))

    units = [(b, p) for b in range(q_ref.shape[0]) for p in range(n_pairs)]

    def scores(u):
        b, p = units[u]
        cols = slice(p * LANES, (p + 1) * LANES)
        qp, kp = q_ref[b, :, cols], k_ref[b, :, cols]
        zero = jnp.zeros_like(qp)
        lhs = jnp.concatenate([jnp.where(low, qp, zero), jnp.where(low, zero, qp)], axis=0)
        s = lax.dot_general(lhs, kp, NT, preferred_element_type=F32)
        qs = jnp.dot(lhs, state_ref[b, p].astype(BF16), preferred_element_type=F32)
        return s, qs, kp

    def decay(u, sc):
        _, p = units[u]
        s, qs, kp = sc
        a = [(s[hh * C_CHUNK:(hh + 1) * C_CHUNK] * dec_ref[0, 2 * p + hh]).astype(BF16) for hh in range(2)]
        kd_t = (kp.astype(F32) * dec_ref[2, p]).T.astype(BF16)
        return a, kd_t, qs

    def values(u, dc):
        b, p = units[u]
        a, kd_t, qs = dc
        outs = []
        for hh in range(2):
            h = 2 * p + hh
            v = v_ref[b, :, h * C_VAL_DIM:(h + 1) * C_VAL_DIM]
            srows = slice(hh * C_KEY_DIM, (hh + 1) * C_KEY_DIM)
            res = jnp.dot(jnp.concatenate([a[hh], kd_t[srows, :]], axis=0), v, preferred_element_type=F32)
            outs.append(res[:C_CHUNK] + qs[hh * C_CHUNK:(hh + 1) * C_CHUNK] * dec_ref[1, h])
            state_ref[b, p, srows, :] = (state_ref[b, p, srows, :] * math.exp(log_g[h] * C_CHUNK)
                                         + res[C_CHUNK:])
        return outs

    def store(u, outs):
        b, p = units[u]
        for hh in range(2):
            h = 2 * p + hh
            vs = slice(h * C_VAL_DIM, (h + 1) * C_VAL_DIM)
            o = outs[hh]
            mu = jnp.mean(o, axis=-1, keepdims=True)
            oc = o - mu
            var = jnp.mean(oc * oc, axis=-1, keepdims=True)
            gate = gate_ref[b, :, vs].astype(F32)
            o_ref[b, :, vs] = (gate * jax.nn.sigmoid(gate) * (oc * lax.rsqrt(var + LN_EPS))).astype(o_ref.dtype)

    _software_pipeline(len(units), scores, decay, values, lambda u, outs, _: store(u, outs), lags=RET_LAGS)


def retention_gated(q, k, v, gate):
    bsz, seq, _ = q.shape
    blk = lambda w: pl.BlockSpec((bsz, C_CHUNK, w), lambda n: (0, n, 0))
    return pl.pallas_call(
        _retention_kernel,
        grid=(seq // C_CHUNK,),
        in_specs=[blk(C_QK), blk(C_QK), blk(C_V), blk(C_V)],
        out_specs=blk(C_V),
        out_shape=jax.ShapeDtypeStruct((bsz, seq, C_V), BF16),
        scratch_shapes=[pltpu.VMEM((bsz, C_HEADS // 2, 2 * C_KEY_DIM, C_VAL_DIM), F32),
                        pltpu.VMEM((3, C_HEADS, C_CHUNK, LANES), F32)],
        compiler_params=_params(1),
        name="retention",
    )(q, k, v, gate)


def _merge_kernel(x_ref, xb_ref, o1_ref, o2_ref, o3_ref, l1_ref, l2_ref, l3_ref, yb_ref, yc_ref,
                  wg_ref, gb_ref, pa_ref, pb_ref, pc_ref, wo_ref, e_ref, g_ref, b_ref, out_ref):
    x = _from_slabs(x_ref)
    xb = xb_ref[...]
    l1, l2, l3 = l1_ref[...], l2_ref[...], l3_ref[...]
    m = jnp.maximum(jnp.maximum(l1, l2), l3)
    e1, e2, e3 = jnp.exp(l1 - m), jnp.exp(l2 - m), jnp.exp(l3 - m)
    inv = 1.0 / (e1 + e2 + e3)
    expand = e_ref[...]
    lane = lax.broadcasted_iota(jnp.int32, l1.shape, 1)

    def per_lane(w):
        w = jnp.where(lane < A_HEADS, w, 0.0)
        hi = w.astype(BF16).astype(F32)
        packed = hi + pltpu.roll(w - hi, LANES // 2, axis=1)
        return jnp.dot(packed.astype(BF16), expand, preferred_element_type=F32)

    o3 = _from_slabs(o3_ref)
    ya = (o3 + per_lane(e1 * inv) * (_from_slabs(o1_ref) - o3)
          + per_lane(e2 * inv) * (_from_slabs(o2_ref) - o3))

    def gate(j):
        cols = slice(j * D_MODEL, (j + 1) * D_MODEL)
        return jax.nn.sigmoid(jnp.dot(xb, wg_ref[:, cols], preferred_element_type=F32) + gb_ref[:, cols])

    merged = gate(0) * jnp.dot(ya.astype(BF16), pa_ref[...], preferred_element_type=F32)
    merged += gate(1) * jnp.dot(_from_slabs(yb_ref).astype(BF16), pb_ref[...], preferred_element_type=F32)
    merged += gate(2) * jnp.dot(yc_ref[...], pc_ref[...], preferred_element_type=F32)
    mix = jnp.dot(merged.astype(BF16), wo_ref[...], preferred_element_type=F32)
    out_ref[...] = _layer_norm(ALPHA * x + mix, g_ref[...], b_ref[...])


def merge_ln(x_slabs, xb, o1, o2, o3, l1, l2, l3, yb, yc, wg, gb, pa, pb, pc, wo, g, b, tm=256):
    m = xb.shape[0]
    head = jnp.arange(A_W) // HEAD_DIM
    expand = ((jnp.arange(LANES)[:, None] % (LANES // 2)) == head[None, :]).astype(BF16)
    row = lambda w: pl.BlockSpec((tm, w), lambda i: (i, 0))
    slabs = lambda w: pl.BlockSpec((w // LANES, tm, LANES), lambda i: (0, i, 0))
    return pl.pallas_call(
        _merge_kernel,
        grid=(m // tm,),
        in_specs=[slabs(D_MODEL), row(D_MODEL), slabs(A_W), slabs(A_W), slabs(A_W),
                  row(LANES), row(LANES), row(LANES), slabs(B_QW), row(C_V),
                  _full(wg.shape), _full(gb.shape), _full(pa.shape),
                  _full(pb.shape), _full(pc.shape), _full(wo.shape), _full(expand.shape),
                  _full((1, D_MODEL)), _full((1, D_MODEL))],
        out_specs=row(D_MODEL),
        out_shape=jax.ShapeDtypeStruct((m, D_MODEL), F32),
        compiler_params=_params(1),
        name="merge_ln",
    )(x_slabs, xb, o1, o2, o3, l1, l2, l3, yb, yc, wg, gb, pa, pb, pc, wo, expand, g, b)


def kernel(x, positions, w_in, gate_bias, attn_sinks, w_proj_a, w_proj_b, w_proj_c, w_out, ffn1_up, ffn1_down, ffn2_up, ffn2_down, ln1_g, ln1_b, ln2_g, ln2_b, ln3_g, ln3_b):
    bsz, seq, _ = x.shape
    m = bsz * seq
    dils = tuple(d for _, d in A_GROUPS)
    tables = {1: rope_tables(positions.reshape(m, 1))}
    tables.update(zip(dils[1:], permute_tables(*tables[1], dils[1:])))
    x = x.reshape(m, D_MODEL)
    a_end = 9 * A_W
    b_end = a_end + B_QW + 2 * B_KVW
    c_end = b_end + 2 * C_QK + 2 * C_V
    qk_scale = HEAD_DIM ** -0.5 * LOG2_E
    seg_a = ((A_W, True, qk_scale), (A_W, True, 1.0))
    seg_b = ((B_QW, True, qk_scale), (B_KVW, True, 1.0))
    seg_c = ((C_QK, True, 1.0), (C_QK, True, C_KEY_DIM ** -0.5), (C_V, False, 1.0), (C_V, False, 1.0))
    row = lambda t: t.reshape(1, -1)
    for l in range(DEPTH):
        x_slabs, xb = ffn_ln(x, ffn1_up[l].astype(BF16), ffn1_down[l].astype(BF16), row(ln1_g[l]), row(ln1_b[l]), True)
        xb_by_dil = dict(zip(dils[1:], permute_tokens(x_slabs, dils[1:])))
        xb_by_dil[1] = xb
        w = w_in[l].astype(BF16)
        outs, lses = [], []
        for gi, (window, dil) in enumerate(A_GROUPS):
            c0 = 3 * gi * A_W
            q, k, vt = proj_rope(xb_by_dil[dil], w[:, c0:c0 + 2 * A_W], w[:, c0 + 2 * A_W:c0 + 3 * A_W].T,
                                 *tables[dil], seg_a, "proj_a")
            o, lse = banded_attention(q, k, vt, window // dil, dil, seq)
            outs.append(o)
            lses.append(lse)
        qb, kb, vtb = proj_rope(xb, w[:, a_end:a_end + B_QW + B_KVW], w[:, a_end + B_QW + B_KVW:b_end].T,
                                *tables[1], seg_b, "proj_b")
        yb, = banded_attention(qb, kb, vtb, B_WINDOW - 1, 1, seq, attn_sinks[l], want_lse=False)
        qc, kc, vc, gc = proj_rope(xb, w[:, b_end:c_end], None, *tables[1], seg_c, "proj_c")
        yc = retention_gated(qc.reshape(bsz, seq, C_QK), kc.reshape(bsz, seq, C_QK),
                             vc.reshape(bsz, seq, C_V), gc.reshape(bsz, seq, C_V))
        x = merge_ln(x_slabs, xb, *outs, *lses, yb, yc.reshape(m, C_V),
                     w[:, c_end:], row(gate_bias[l]), w_proj_a[l].astype(BF16), w_proj_b[l].astype(BF16),
                     w_proj_c[l].astype(BF16), w_out[l].astype(BF16), row(ln2_g[l]), row(ln2_b[l]))
        x = ffn_ln(x, ffn2_up[l].astype(BF16), ffn2_down[l].astype(BF16), row(ln3_g[l]), row(ln3_b[l]), False)
    return x.reshape(bsz, seq, D_MODEL)
```

```python
import functools
import math

import jax
import jax.numpy as jnp
from jax import lax
from jax.experimental import pallas as pl
from jax.experimental.pallas import tpu as pltpu

D_MODEL = 1024
DEPTH = 2
HEAD_DIM = 64
BLOCK = 128
A_GROUPS = ((128, 1), (512, 4), (2048, 16))
A_HEADS = 12
A_W = A_HEADS * HEAD_DIM
B_Q_HEADS = 16
B_KV_HEADS = 2
B_WINDOW = 128
B_QW = B_Q_HEADS * HEAD_DIM
B_KVW = B_KV_HEADS * HEAD_DIM
C_HEADS = 8
C_KEY_DIM = 64
C_VAL_DIM = 128
C_CHUNK = 128
C_QK = C_HEADS * C_KEY_DIM
C_V = C_HEADS * C_VAL_DIM
D_FF = 2816
ROPE_THETA = 10000.0
LN_EPS = 1e-5
ALPHA = (2.0 * DEPTH) ** 0.25

LANES = 128
MXU_COLS = 256
FF_CHUNK = 256
FFN_SUB = 512
MERGE_SUB = 256
SPAN = 2048
PLAIN_SPAN = 512
ATTN_UNITS_BLOCKS = 4
ATTN_LAGS = (3, 4, 7)
RET_LAGS = (1, 2, 3)
NEG_BIG = -1e30
ONES_ROWS = 16
LOG2_E = math.log2(math.e)
LN_2 = math.log(2.0)
VMEM_LIMIT = 56 * 1024 * 1024
N_SLABS = D_MODEL // LANES

F32 = jnp.float32
BF16 = jnp.bfloat16
NT = (((1,), (1,)), ((), ()))


def _params(n_axes):
    return pltpu.CompilerParams(dimension_semantics=("arbitrary",) * n_axes,
                                vmem_limit_bytes=VMEM_LIMIT)


def _layer_norm(y, g, b):
    mu = jnp.mean(y, axis=-1, keepdims=True)
    yc = y - mu
    var = jnp.mean(yc * yc, axis=-1, keepdims=True)
    return yc * lax.rsqrt(var + LN_EPS) * g + b


def _full(shape):
    nd = len(shape)
    return pl.BlockSpec(shape, lambda *_: (0,) * nd, pipeline_mode=pl.Buffered(1))


def _software_pipeline(n, stage_a, stage_b, stage_c, stage_d, lags=(1, 2, 3)):
    a_val, b_val, c_val = {}, {}, {}
    lag_b, lag_c, lag_d = lags
    for t in range(n + lag_d):
        if t < n:
            a_val[t] = stage_a(t)
        if 0 <= t - lag_b < n:
            b_val[t - lag_b] = stage_b(t - lag_b, a_val.pop(t - lag_b))
        if 0 <= t - lag_c < n:
            c_val[t - lag_c] = stage_c(t - lag_c, b_val[t - lag_c])
        if 0 <= t - lag_d < n:
            stage_d(t - lag_d, c_val.pop(t - lag_d), b_val.pop(t - lag_d))


def _rope_table_kernel(pos_ref, inv_ref, sign_ref, cos_ref, sin_ref):
    ang = pos_ref[...].astype(F32) * inv_ref[...]
    cos_ref[...] = jnp.cos(ang)
    sin_ref[...] = jnp.sin(ang) * sign_ref[...]


def rope_tables(pos_col, tm=1024):
    m = pos_col.shape[0]
    half = HEAD_DIM // 2
    inv = ROPE_THETA ** (-jnp.arange(half, dtype=F32) / half)
    inv_row = jnp.tile(inv, LANES // half)[None, :]
    lane = jnp.arange(LANES)
    sign_row = jnp.where(lane % HEAD_DIM < half, -1.0, 1.0).astype(F32)[None, :]
    return pl.pallas_call(
        _rope_table_kernel,
        grid=(m // tm,),
        in_specs=[pl.BlockSpec((tm, 1), lambda i: (i, 0)), _full((1, LANES)), _full((1, LANES))],
        out_specs=[pl.BlockSpec((tm, LANES), lambda i: (i, 0))] * 2,
        out_shape=[jax.ShapeDtypeStruct((m, LANES), F32)] * 2,
        compiler_params=_params(1),
        name="rope_table",
    )(pos_col, inv_row, sign_row)


def _ffn_kernel(x_ref, wu_ref, wd_ref, g_ref, b_ref, *out_refs, slab_out):
    n_chunks = D_FF // FF_CHUNK
    for sub in range(x_ref.shape[0] // FFN_SUB):
        rows = slice(sub * FFN_SUB, (sub + 1) * FFN_SUB)
        x = x_ref[rows, :]
        xb = x.astype(BF16)

        def hidden(c, xb=xb):
            lo = c * FF_CHUNK
            a = jnp.dot(xb, wu_ref[:, lo:lo + FF_CHUNK], preferred_element_type=F32)
            b = jnp.dot(xb, wu_ref[:, D_FF + lo:D_FF + lo + FF_CHUNK], preferred_element_type=F32)
            return (a * jax.nn.sigmoid(a) * b).astype(BF16)

        h = hidden(0)
        acc = None
        for c in range(n_chunks):
            h_next = hidden(c + 1) if c + 1 < n_chunks else None
            d = jnp.dot(h, wd_ref[c * FF_CHUNK:(c + 1) * FF_CHUNK, :], preferred_element_type=F32)
            acc = d if acc is None else acc + d
            h = h_next
        y = _layer_norm(ALPHA * x + 0.5 * acc, g_ref[...], b_ref[...])
        if slab_out:
            slab_ref, bf_ref = out_refs
            for j in range(N_SLABS):
                slab_ref[j, rows, :] = y[:, j * LANES:(j + 1) * LANES]
            bf_ref[rows, :] = y.astype(BF16)
        else:
            out_refs[0][rows, :] = y


def ffn_ln(x, wu, wd, g, b, slab_out, tm=1024):
    m = x.shape[0]
    row = pl.BlockSpec((tm, D_MODEL), lambda i: (i, 0))
    if slab_out:
        out_specs = [pl.BlockSpec((N_SLABS, tm, LANES), lambda i: (0, i, 0)), row]
        out_shape = [jax.ShapeDtypeStruct((N_SLABS, m, LANES), F32), jax.ShapeDtypeStruct((m, D_MODEL), BF16)]
    else:
        out_specs = row
        out_shape = jax.ShapeDtypeStruct((m, D_MODEL), F32)
    return pl.pallas_call(
        functools.partial(_ffn_kernel, slab_out=slab_out),
        grid=(m // tm,),
        in_specs=[row, _full(wu.shape), _full(wd.shape), _full((1, D_MODEL)), _full((1, D_MODEL))],
        out_specs=out_specs,
        out_shape=out_shape,
        compiler_params=_params(1),
        name="ffn_ln",
    )(x, wu, wd, g, b)


def _permute_kernel(x_ref, *out_refs, dils):
    for o_ref, d in zip(out_refs, dils):
        chunk = SPAN // d
        for r in range(d):
            for j in range(N_SLABS):
                o_ref[r * chunk:(r + 1) * chunk, j * LANES:(j + 1) * LANES] = (
                    x_ref[j, pl.ds(r, chunk, stride=d), :].astype(BF16))


def permute_tokens(x_slabs, dils):
    m = x_slabs.shape[1]
    return pl.pallas_call(
        functools.partial(_permute_kernel, dils=dils),
        grid=(m // SPAN,),
        in_specs=[pl.BlockSpec((N_SLABS, SPAN, LANES), lambda i: (0, i, 0))],
        out_specs=[pl.BlockSpec((SPAN, D_MODEL), lambda i: (i, 0)) for _ in dils],
        out_shape=[jax.ShapeDtypeStruct((m, D_MODEL), BF16) for _ in dils],
        compiler_params=_params(1),
        name="permute_tokens",
    )(x_slabs)


def _permute_table_kernel(cos_ref, sin_ref, *out_refs, dils):
    outs = iter(out_refs)
    for d in dils:
        chunk = SPAN // d
        for t_ref in (cos_ref, sin_ref):
            o_ref = next(outs)
            for r in range(d):
                o_ref[r * chunk:(r + 1) * chunk, :] = t_ref[pl.ds(r, chunk, stride=d), :]


def permute_tables(cos, sin_signed, dils):
    m = cos.shape[0]
    blk = pl.BlockSpec((SPAN, LANES), lambda i: (i, 0))
    outs = pl.pallas_call(
        functools.partial(_permute_table_kernel, dils=dils),
        grid=(m // SPAN,),
        in_specs=[blk, blk],
        out_specs=[blk] * (2 * len(dils)),
        out_shape=[jax.ShapeDtypeStruct((m, LANES), F32)] * (2 * len(dils)),
        compiler_params=_params(1),
        name="permute_tables",
    )(cos, sin_signed)
    return [(outs[2 * i], outs[2 * i + 1]) for i in range(len(dils))]


def _rope_slab(y, cos, sin_signed, first_half):
    partner = jnp.where(first_half, pltpu.roll(y, LANES - HEAD_DIM // 2, axis=1),
                        pltpu.roll(y, HEAD_DIM // 2, axis=1))
    return y * cos + partner * sin_signed


def _proj_kernel(x_ref, w_ref, wt_ref, cos_ref, sin_ref, *out_refs, segs, t_width):
    xb = x_ref[...]
    cos = cos_ref[...]
    sin_signed = sin_ref[...]
    lane = lax.broadcasted_iota(jnp.int32, cos.shape, 1)
    first_half = (lane % HEAD_DIM) < (HEAD_DIM // 2)
    col = 0
    for o_ref, (width, rope, scale) in zip(out_refs, segs):
        for c0 in range(0, width, MXU_COLS):
            cw = min(MXU_COLS, width - c0)
            y2 = jnp.dot(xb, w_ref[:, col + c0:col + c0 + cw], preferred_element_type=F32)
            for s0 in range(0, cw, LANES):
                y = y2[:, s0:s0 + LANES]
                if rope:
                    y = _rope_slab(y, cos, sin_signed, first_half)
                if scale != 1.0:
                    y = y * scale
                o_ref[:, c0 + s0:c0 + s0 + LANES] = y.astype(o_ref.dtype)
        col += width
    if t_width:
        t_ref = out_refs[len(segs)]
        for c0 in range(0, t_width, MXU_COLS):
            cw = min(MXU_COLS, t_width - c0)
            t_ref[c0:c0 + cw, :] = lax.dot_general(wt_ref[c0:c0 + cw, :], xb, NT,
                                                   preferred_element_type=F32).astype(t_ref.dtype)


def proj_rope(xb, w, wt, cos, sin_signed, segs, name, tm=512):
    m = xb.shape[0]
    assert w.shape[1] == sum(s[0] for s in segs)
    t_width = 0 if wt is None else wt.shape[0]
    if wt is None:
        wt = jnp.zeros((8, D_MODEL), BF16)
    row = lambda wd: pl.BlockSpec((tm, wd), lambda i: (i, 0))
    out_specs = [row(s[0]) for s in segs]
    out_shape = [jax.ShapeDtypeStruct((m, s[0]), BF16) for s in segs]
    if t_width:
        out_specs.append(pl.BlockSpec((t_width, tm), lambda i: (0, i)))
        out_shape.append(jax.ShapeDtypeStruct((t_width, m), BF16))
    return pl.pallas_call(
        functools.partial(_proj_kernel, segs=segs, t_width=t_width),
        grid=(m // tm,),
        in_specs=[row(D_MODEL), _full(w.shape), _full(wt.shape), row(LANES), row(LANES)],
        out_specs=out_specs,
        out_shape=out_shape,
        compiler_params=_params(1),
        name=name,
    )(xb, w, wt, cos, sin_signed)


def _attn_kernel(*refs, n_heads, gqa, max_dist, dil, n_res, n_blocks, spans_per_batch, has_sink, want_lse):
    refs = list(refs)
    sink_ref = refs.pop(0) if has_sink else None
    q_ref, kc_ref, kh_ref, vc_ref, vh_ref, o_ref = refs[:6]
    lse_ref = refs[6] if want_lse else None
    lse_scr, o_scr = refs[-2:]
    g = pl.program_id(0)
    r = pl.program_id(1)
    first_has_prev = (g % spans_per_batch) > 0
    key = lax.broadcasted_iota(jnp.int32, (2 * BLOCK, 2 * BLOCK), 0)
    qry = lax.broadcasted_iota(jnp.int32, (2 * BLOCK, 2 * BLOCK), 1) % BLOCK
    in_band_prev = jnp.logical_and(key < BLOCK, key >= qry + (BLOCK - max_dist))
    in_band_cur = jnp.logical_and(key >= BLOCK, key - BLOCK <= qry)
    bias_on = jnp.where(jnp.logical_or(in_band_prev, in_band_cur), 0.0, NEG_BIG)
    bias_first = jnp.where(jnp.logical_or(jnp.logical_and(in_band_prev, first_has_prev), in_band_cur),
                           0.0, NEG_BIG)
    low = lax.broadcasted_iota(jnp.int32, (BLOCK, LANES), 1) < HEAD_DIM
    lane2 = lax.broadcasted_iota(jnp.int32, (1, 2 * BLOCK), 1)
    ones_rows = jnp.ones((ONES_ROWS, 2 * BLOCK), BF16)
    n_pairs = n_heads // 2
    rep = n_heads // (kc_ref.shape[1] // HEAD_DIM)
    units = [(j, i, p) for j in range(n_res) for i in range(n_blocks) for p in range(n_pairs)]

    def rows_of(j, i):
        return slice((j * n_blocks + i) * BLOCK, (j * n_blocks + i + 1) * BLOCK)

    def prev_and_cur(j, i):
        return slice((j * n_blocks + i - 1) * BLOCK, (j * n_blocks + i + 1) * BLOCK)

    def scores(u):
        j, i, p = units[u]
        rows = rows_of(j, i)
        qp = q_ref[rows, p * LANES:(p + 1) * LANES]
        zero = jnp.zeros_like(qp)
        if gqa:
            kv = (2 * p) // rep
            qr = pltpu.roll(qp, HEAD_DIM, axis=1)
            first, second = (qp, qr) if kv == 0 else (qr, qp)
            keep = low if kv == 0 else jnp.logical_not(low)
            rhs = jnp.concatenate([jnp.where(keep, first, zero), jnp.where(keep, second, zero)], axis=0)
            kcols = slice(0, LANES)
        else:
            rhs = jnp.concatenate([jnp.where(low, qp, zero), jnp.where(low, zero, qp)], axis=0)
            kcols = slice(p * LANES, (p + 1) * LANES)
        if i == 0:
            keys = jnp.concatenate([kh_ref[j * BLOCK:(j + 1) * BLOCK, kcols], kc_ref[rows, kcols]], axis=0)
            bias = bias_first
        else:
            keys, bias = kc_ref[prev_and_cur(j, i), kcols], bias_on
        return lax.dot_general(keys, rhs, NT, preferred_element_type=F32) + bias

    def softmax(u, s):
        p = units[u][2]
        m = jnp.max(s, axis=0, keepdims=True)
        sink_row = None
        if has_sink:
            sink_row = jnp.where(lane2 < BLOCK, sink_ref[2 * p], sink_ref[2 * p + 1]) * LOG2_E
            m = jnp.maximum(m, sink_row)
        return jnp.exp2(s - m).astype(BF16), m, sink_row

    def values(u, sm):
        j, i, p = units[u]
        vrows = (slice(((2 * p) // rep) * HEAD_DIM, ((2 * p) // rep + 1) * HEAD_DIM) if gqa
                 else slice(p * LANES, (p + 1) * LANES))
        if i == 0:
            vt = jnp.concatenate([vh_ref[vrows, j * BLOCK:(j + 1) * BLOCK], vc_ref[vrows, rows_of(j, i)]], axis=1)
        else:
            vt = vc_ref[vrows, prev_and_cur(j, i)]
        return jnp.dot(jnp.concatenate([vt, ones_rows], axis=0), sm[0], preferred_element_type=F32)

    def store(u, res, sm):
        j, i, p = units[u]
        _, m, sink_row = sm
        n_feat = res.shape[0] - ONES_ROWS
        den = res[n_feat:n_feat + 1, :]
        if has_sink:
            den = den + jnp.exp2(sink_row - m)
        inv = 1.0 / den
        second = slice(0, HEAD_DIM) if gqa else slice(HEAD_DIM, 2 * HEAD_DIM)
        o_pair = jnp.concatenate([res[:HEAD_DIM, :BLOCK] * inv[:, :BLOCK],
                                  res[second, BLOCK:] * inv[:, BLOCK:]], axis=0)
        if dil == 1:
            dst = rows_of(j, i)
            o_ref[dst, p * LANES:(p + 1) * LANES] = o_pair.T.astype(o_ref.dtype)
        else:
            dst = pl.ds(r * n_res + j + dil * BLOCK * i, BLOCK, stride=dil)
            o_scr[p, dst, :] = o_pair.T
        if want_lse:
            lse = (m + jnp.log2(den)) * LN_2
            if p == 0:
                lse_scr[...] = jnp.zeros_like(lse_scr)
            lse_scr[2 * p:2 * p + 1, :] = lse[:, :BLOCK]
            lse_scr[2 * p + 1:2 * p + 2, :] = lse[:, BLOCK:]
            if p == n_pairs - 1:
                lse_ref[dst, :] = lse_scr[...].T

    _software_pipeline(len(units), scores, softmax, values, store, lags=ATTN_LAGS)

    if dil > 1:
        @pl.when(r == pl.num_programs(1) - 1)
        def _():
            for p in range(n_pairs):
                o_ref[:, p * LANES:(p + 1) * LANES] = o_scr[p].astype(o_ref.dtype)


def banded_attention(q, k, vt, max_dist, dil, seq, sink=None, want_lse=True):
    m, qw = q.shape
    kw = k.shape[1]
    n_heads = qw // HEAD_DIM
    gqa = kw != qw
    span = SPAN if dil > 1 else PLAIN_SPAN
    chunk = span // dil
    n_blocks = chunk // BLOCK
    n_res = ATTN_UNITS_BLOCKS // n_blocks
    assert n_res == 1 or n_blocks == 1
    spans_per_batch = seq // span
    steps = dil // n_res
    rows = chunk * n_res
    halo = BLOCK * n_res

    def cur(g, r):
        return (g * steps + r, 0)

    def cur_t(g, r):
        return (0, g * steps + r)

    def halo_block(g, r):
        return jnp.maximum((((g - 1) * dil + (r + 1) * n_res) * chunk - halo) // halo, 0)

    in_specs = [pl.BlockSpec((rows, qw), cur),
                pl.BlockSpec((rows, kw), cur), pl.BlockSpec((halo, kw), lambda g, r: (halo_block(g, r), 0)),
                pl.BlockSpec((kw, rows), cur_t), pl.BlockSpec((kw, halo), lambda g, r: (0, halo_block(g, r)))]
    args = [q, k, k, vt, vt]
    if sink is not None:
        in_specs = [pl.BlockSpec(memory_space=pltpu.SMEM)] + in_specs
        args = [sink] + args
    out_specs = [pl.BlockSpec((span, qw), lambda g, r: (g, 0))]
    out_shape = [jax.ShapeDtypeStruct((m, qw), BF16)]
    o_scr_rows = span if dil > 1 else 8
    if want_lse:
        out_specs.append(pl.BlockSpec((span, LANES), lambda g, r: (g, 0)))
        out_shape.append(jax.ShapeDtypeStruct((m, LANES), F32))
    return pl.pallas_call(
        functools.partial(_attn_kernel, n_heads=n_heads, gqa=gqa, max_dist=max_dist, dil=dil, n_res=n_res,
                          n_blocks=n_blocks, spans_per_batch=spans_per_batch,
                          has_sink=sink is not None, want_lse=want_lse),
        grid=(m // span, steps),
        in_specs=in_specs,
        out_specs=out_specs,
        out_shape=out_shape,
        scratch_shapes=[pltpu.VMEM((LANES, BLOCK), F32), pltpu.VMEM((qw // LANES, o_scr_rows, LANES), F32)],
        compiler_params=_params(2),
        name="banded_attention",
    )(*args)


def _retention_kernel(q_ref, k_ref, v_ref, gate_ref, o_ref, state_ref, dec_ref):
    n = pl.program_id(0)
    n_pairs = C_HEADS // 2
    log_g = [math.log1p(-2.0 ** (-5.0 - h)) for h in range(C_HEADS)]
    row = lax.broadcasted_iota(jnp.int32, (C_CHUNK, LANES), 0)
    lane = lax.broadcasted_iota(jnp.int32, (C_CHUNK, LANES), 1)
    low = lane < C_KEY_DIM

    @pl.when(n == 0)
    def _():
        state_ref[...] = jnp.zeros_like(state_ref)
        rel = (row - lane).astype(F32)
        idx = row.astype(F32)
        for h in range(C_HEADS):
            dec_ref[0, h] = jnp.where(rel >= 0, jnp.exp(log_g[h] * jnp.maximum(rel, 0.0)), 0.0)
            dec_ref[1, h] = jnp.exp(log_g[h] * (idx + 1.0))
        for p in range(n_pairs):
            lg = jnp.where(low, log_g[2 * p], log_g[2 * p + 1])
            dec_ref[2, p] = jnp.exp(lg * (C_CHUNK - 1.0 - idx))

    units = [(b, p) for b in range(q_ref.shape[0]) for p in range(n_pairs)]

    def scores(u):
        b, p = units[u]
        cols = slice(p * LANES, (p + 1) * LANES)
        qp, kp = q_ref[b, :, cols], k_ref[b, :, cols]
        zero = jnp.zeros_like(qp)
        lhs = jnp.concatenate([jnp.where(low, qp, zero), jnp.where(low, zero, qp)], axis=0)
        s = lax.dot_general(lhs, kp, NT, preferred_element_type=F32)
        qs = jnp.dot(lhs, state_ref[b, p].astype(BF16), preferred_element_type=F32)
        return s, qs, kp

    def decay(u, sc):
        _, p = units[u]
        s, qs, kp = sc
        a = [(s[hh * C_CHUNK:(hh + 1) * C_CHUNK] * dec_ref[0, 2 * p + hh]).astype(BF16) for hh in range(2)]
        kd_t = (kp.astype(F32) * dec_ref[2, p]).T.astype(BF16)
        return a, kd_t, qs

    def values(u, dc):
        b, p = units[u]
        a, kd_t, qs = dc
        outs = []
        for hh in range(2):
            h = 2 * p + hh
            v = v_ref[b, :, h * C_VAL_DIM:(h + 1) * C_VAL_DIM]
            srows = slice(hh * C_KEY_DIM, (hh + 1) * C_KEY_DIM)
            res = jnp.dot(jnp.concatenate([a[hh], kd_t[srows, :]], axis=0), v, preferred_element_type=F32)
            outs.append(res[:C_CHUNK] + qs[hh * C_CHUNK:(hh + 1) * C_CHUNK] * dec_ref[1, h])
            state_ref[b, p, srows, :] = (state_ref[b, p, srows, :] * math.exp(log_g[h] * C_CHUNK)
                                         + res[C_CHUNK:])
        return outs

    def store(u, outs):
        b, p = units[u]
        for hh in range(2):
            h = 2 * p + hh
            vs = slice(h * C_VAL_DIM, (h + 1) * C_VAL_DIM)
            o = outs[hh]
            mu = jnp.mean(o, axis=-1, keepdims=True)
            oc = o - mu
            var = jnp.mean(oc * oc, axis=-1, keepdims=True)
            gate = gate_ref[b, :, vs].astype(F32)
            o_ref[b, :, vs] = (gate * jax.nn.sigmoid(gate) * (oc * lax.rsqrt(var + LN_EPS))).astype(o_ref.dtype)

    _software_pipeline(len(units), scores, decay, values, lambda u, outs, _: store(u, outs), lags=RET_LAGS)


def retention_gated(q, k, v, gate):
    bsz, seq, _ = q.shape
    blk = lambda w: pl.BlockSpec((bsz, C_CHUNK, w), lambda n: (0, n, 0))
    return pl.pallas_call(
        _retention_kernel,
        grid=(seq // C_CHUNK,),
        in_specs=[blk(C_QK), blk(C_QK), blk(C_V), blk(C_V)],
        out_specs=blk(C_V),
        out_shape=jax.ShapeDtypeStruct((bsz, seq, C_V), BF16),
        scratch_shapes=[pltpu.VMEM((bsz, C_HEADS // 2, 2 * C_KEY_DIM, C_VAL_DIM), F32),
                        pltpu.VMEM((3, C_HEADS, C_CHUNK, LANES), F32)],
        compiler_params=_params(1),
        name="retention",
    )(q, k, v, gate)


def _merge_kernel(x_ref, xb_ref, o1_ref, o2_ref, o3_ref, l1_ref, l2_ref, l3_ref, yb_ref, yc_ref,
                  wg_ref, gb_ref, pa_ref, pb_ref, pc_ref, wo_ref, e_ref, g_ref, b_ref, out_ref):
    expand = e_ref[...]
    lane = lax.broadcasted_iota(jnp.int32, (MERGE_SUB, LANES), 1)

    def per_lane(w):
        w = jnp.where(lane < A_HEADS, w, 0.0)
        hi = w.astype(BF16).astype(F32)
        packed = hi + pltpu.roll(w - hi, LANES // 2, axis=1)
        return jnp.dot(packed.astype(BF16), expand, preferred_element_type=F32)

    for sub in range(xb_ref.shape[0] // MERGE_SUB):
        rows = slice(sub * MERGE_SUB, (sub + 1) * MERGE_SUB)
        x = jnp.concatenate([x_ref[j, rows, :] for j in range(N_SLABS)], axis=1)
        xb = xb_ref[rows, :]
        l1, l2, l3 = l1_ref[rows, :], l2_ref[rows, :], l3_ref[rows, :]
        m = jnp.maximum(jnp.maximum(l1, l2), l3)
        e1, e2, e3 = jnp.exp(l1 - m), jnp.exp(l2 - m), jnp.exp(l3 - m)
        inv = 1.0 / (e1 + e2 + e3)
        o3 = o3_ref[rows, :].astype(F32)
        ya = (o3 + per_lane(e1 * inv) * (o1_ref[rows, :].astype(F32) - o3)
              + per_lane(e2 * inv) * (o2_ref[rows, :].astype(F32) - o3))

        def gate(j, xb=xb):
            cols = slice(j * D_MODEL, (j + 1) * D_MODEL)
            return jax.nn.sigmoid(jnp.dot(xb, wg_ref[:, cols], preferred_element_type=F32) + gb_ref[:, cols])

        merged = gate(0) * jnp.dot(ya.astype(BF16), pa_ref[...], preferred_element_type=F32)
        merged += gate(1) * jnp.dot(yb_ref[rows, :], pb_ref[...], preferred_element_type=F32)
        merged += gate(2) * jnp.dot(yc_ref[rows, :], pc_ref[...], preferred_element_type=F32)
        mix = jnp.dot(merged.astype(BF16), wo_ref[...], preferred_element_type=F32)
        out_ref[rows, :] = _layer_norm(ALPHA * x + mix, g_ref[...], b_ref[...])


def merge_ln(x_slabs, xb, o1, o2, o3, l1, l2, l3, yb, yc, wg, gb, pa, pb, pc, wo, g, b, tm=512):
    m = xb.shape[0]
    head = jnp.arange(A_W) // HEAD_DIM
    expand = ((jnp.arange(LANES)[:, None] % (LANES // 2)) == head[None, :]).astype(BF16)
    row = lambda w: pl.BlockSpec((tm, w), lambda i: (i, 0))
    return pl.pallas_call(
        _merge_kernel,
        grid=(m // tm,),
        in_specs=[pl.BlockSpec((N_SLABS, tm, LANES), lambda i: (0, i, 0)), row(D_MODEL),
                  row(A_W), row(A_W), row(A_W), row(LANES), row(LANES), row(LANES), row(B_QW), row(C_V),
                  _full(wg.shape), _full(gb.shape), _full(pa.shape),
                  _full(pb.shape), _full(pc.shape), _full(wo.shape), _full(expand.shape),
                  _full((1, D_MODEL)), _full((1, D_MODEL))],
        out_specs=row(D_MODEL),
        out_shape=jax.ShapeDtypeStruct((m, D_MODEL), F32),
        compiler_params=_params(1),
        name="merge_ln",
    )(x_slabs, xb, o1, o2, o3, l1, l2, l3, yb, yc, wg, gb, pa, pb, pc, wo, expand, g, b)


def kernel(x, positions, w_in, gate_bias, attn_sinks, w_proj_a, w_proj_b, w_proj_c, w_out, ffn1_up, ffn1_down, ffn2_up, ffn2_down, ln1_g, ln1_b, ln2_g, ln2_b, ln3_g, ln3_b):
    bsz, seq, _ = x.shape
    m = bsz * seq
    dils = tuple(d for _, d in A_GROUPS)
    tables = {1: rope_tables(positions.reshape(m, 1))}
    tables.update(zip(dils[1:], permute_tables(*tables[1], dils[1:])))
    x = x.reshape(m, D_MODEL)
    a_end = 9 * A_W
    b_end = a_end + B_QW + 2 * B_KVW
    c_end = b_end + 2 * C_QK + 2 * C_V
    qk_scale = HEAD_DIM ** -0.5 * LOG2_E
    seg_a = ((A_W, True, qk_scale), (A_W, True, 1.0))
    seg_b = ((B_QW, True, qk_scale), (B_KVW, True, 1.0))
    seg_c = ((C_QK, True, 1.0), (C_QK, True, C_KEY_DIM ** -0.5), (C_V, False, 1.0), (C_V, False, 1.0))
    row = lambda t: t.reshape(1, -1)
    for l in range(DEPTH):
        x_slabs, xb = ffn_ln(x, ffn1_up[l].astype(BF16), ffn1_down[l].astype(BF16), row(ln1_g[l]), row(ln1_b[l]), True)
        xb_by_dil = dict(zip(dils[1:], permute_tokens(x_slabs, dils[1:])))
        xb_by_dil[1] = xb
        w = lambda lo, hi: w_in[l, :, lo:hi].astype(BF16)
        outs, lses = [], []
        for gi, (window, dil) in enumerate(A_GROUPS):
            c0 = 3 * gi * A_W
            q, k, vt = proj_rope(xb_by_dil[dil], w(c0, c0 + 2 * A_W), w(c0 + 2 * A_W, c0 + 3 * A_W).T,
                                 *tables[dil], seg_a, "proj_a")
            o, lse = banded_attention(q, k, vt, window // dil, dil, seq)
            outs.append(o)
            lses.append(lse)
        qb, kb, vtb = proj_rope(xb, w(a_end, a_end + B_QW + B_KVW), w(a_end + B_QW + B_KVW, b_end).T,
                                *tables[1], seg_b, "proj_b")
        yb, = banded_attention(qb, kb, vtb, B_WINDOW - 1, 1, seq, attn_sinks[l], want_lse=False)
        qc, kc, vc, gc = proj_rope(xb, w(b_end, c_end), None, *tables[1], seg_c, "proj_c")
        yc = retention_gated(qc.reshape(bsz, seq, C_QK), kc.reshape(bsz, seq, C_QK),
                             vc.reshape(bsz, seq, C_V), gc.reshape(bsz, seq, C_V))
        x = merge_ln(x_slabs, xb, *outs, *lses, yb, yc.reshape(m, C_V),
                     w(c_end, c_end + 3 * D_MODEL), row(gate_bias[l]), w_proj_a[l].astype(BF16),
                     w_proj_b[l].astype(BF16), w_proj_c[l].astype(BF16), w_out[l].astype(BF16),
                     row(ln2_g[l]), row(ln2_b[l]))
        x = ffn_ln(x, ffn2_up[l].astype(BF16), ffn2_down[l].astype(BF16), row(ln3_g[l]), row(ln3_b[l]), False)
    return x.reshape(bsz, seq, D_MODEL)
```

```python
import functools
import math

import jax
import jax.numpy as jnp
from jax import lax
from jax.experimental import pallas as pl
from jax.experimental.pallas import tpu as pltpu

D_MODEL = 1024
DEPTH = 2
HEAD_DIM = 64
BLOCK = 128
A_GROUPS = ((128, 1), (512, 4), (2048, 16))
A_HEADS = 12
A_W = A_HEADS * HEAD_DIM
B_Q_HEADS = 16
B_KV_HEADS = 2
B_WINDOW = 128
B_QW = B_Q_HEADS * HEAD_DIM
B_KVW = B_KV_HEADS * HEAD_DIM
C_HEADS = 8
C_KEY_DIM = 64
C_VAL_DIM = 128
C_CHUNK = 128
C_QK = C_HEADS * C_KEY_DIM
C_V = C_HEADS * C_VAL_DIM
D_FF = 2816
ROPE_THETA = 10000.0
LN_EPS = 1e-5
ALPHA = (2.0 * DEPTH) ** 0.25

LANES = 128
MXU_COLS = 256
FF_CHUNK = 256
FFN_SUB = 512
MERGE_SUB = 256
SPAN = 2048
PLAIN_SPAN = 1024
ATTN_UNITS_BLOCKS = 8
ATTN_LAGS = (3, 4, 7)
RET_LAGS = (1, 2, 3)
NEG_BIG = -1e30
ONES_ROWS = 16
LOG2_E = math.log2(math.e)
LN_2 = math.log(2.0)
VMEM_LIMIT = 56 * 1024 * 1024
N_SLABS = D_MODEL // LANES

F32 = jnp.float32
BF16 = jnp.bfloat16
NT = (((1,), (1,)), ((), ()))


def _params(n_axes):
    return pltpu.CompilerParams(dimension_semantics=("arbitrary",) * n_axes,
                                vmem_limit_bytes=VMEM_LIMIT)


def _layer_norm(y, g, b):
    mu = jnp.mean(y, axis=-1, keepdims=True)
    yc = y - mu
    var = jnp.mean(yc * yc, axis=-1, keepdims=True)
    return yc * lax.rsqrt(var + LN_EPS) * g + b


def _full(shape):
    nd = len(shape)
    return pl.BlockSpec(shape, lambda *_: (0,) * nd, pipeline_mode=pl.Buffered(1))


def _software_pipeline(n, stage_a, stage_b, stage_c, stage_d, lags=(1, 2, 3)):
    a_val, b_val, c_val = {}, {}, {}
    lag_b, lag_c, lag_d = lags
    for t in range(n + lag_d):
        if t < n:
            a_val[t] = stage_a(t)
        if 0 <= t - lag_b < n:
            b_val[t - lag_b] = stage_b(t - lag_b, a_val.pop(t - lag_b))
        if 0 <= t - lag_c < n:
            c_val[t - lag_c] = stage_c(t - lag_c, b_val[t - lag_c])
        if 0 <= t - lag_d < n:
            stage_d(t - lag_d, c_val.pop(t - lag_d), b_val.pop(t - lag_d))


def _rope_table_kernel(pos_ref, inv_ref, sign_ref, cos_ref, sin_ref):
    ang = pos_ref[...].astype(F32) * inv_ref[...]
    cos_ref[...] = jnp.cos(ang)
    sin_ref[...] = jnp.sin(ang) * sign_ref[...]


def rope_tables(pos_col, tm=1024):
    m = pos_col.shape[0]
    half = HEAD_DIM // 2
    inv = ROPE_THETA ** (-jnp.arange(half, dtype=F32) / half)
    inv_row = jnp.tile(inv, LANES // half)[None, :]
    lane = jnp.arange(LANES)
    sign_row = jnp.where(lane % HEAD_DIM < half, -1.0, 1.0).astype(F32)[None, :]
    return pl.pallas_call(
        _rope_table_kernel,
        grid=(m // tm,),
        in_specs=[pl.BlockSpec((tm, 1), lambda i: (i, 0)), _full((1, LANES)), _full((1, LANES))],
        out_specs=[pl.BlockSpec((tm, LANES), lambda i: (i, 0))] * 2,
        out_shape=[jax.ShapeDtypeStruct((m, LANES), F32)] * 2,
        compiler_params=_params(1),
        name="rope_table",
    )(pos_col, inv_row, sign_row)


def _ffn_kernel(x_ref, wu_ref, wd_ref, g_ref, b_ref, *out_refs, slab_out):
    n_chunks = D_FF // FF_CHUNK
    for sub in range(x_ref.shape[0] // FFN_SUB):
        rows = slice(sub * FFN_SUB, (sub + 1) * FFN_SUB)
        x = x_ref[rows, :]
        xb = x.astype(BF16)

        def hidden(c, xb=xb):
            lo = c * FF_CHUNK
            a = jnp.dot(xb, wu_ref[:, lo:lo + FF_CHUNK], preferred_element_type=F32)
            b = jnp.dot(xb, wu_ref[:, D_FF + lo:D_FF + lo + FF_CHUNK], preferred_element_type=F32)
            return (a * jax.nn.sigmoid(a) * b).astype(BF16)

        h = hidden(0)
        acc = None
        for c in range(n_chunks):
            h_next = hidden(c + 1) if c + 1 < n_chunks else None
            d = jnp.dot(h, wd_ref[c * FF_CHUNK:(c + 1) * FF_CHUNK, :], preferred_element_type=F32)
            acc = d if acc is None else acc + d
            h = h_next
        y = _layer_norm(ALPHA * x + 0.5 * acc, g_ref[...], b_ref[...])
        if slab_out:
            slab_ref, bf_ref = out_refs
            for j in range(N_SLABS):
                slab_ref[j, rows, :] = y[:, j * LANES:(j + 1) * LANES]
            bf_ref[rows, :] = y.astype(BF16)
        else:
            out_refs[0][rows, :] = y


def ffn_ln(x, wu, wd, g, b, slab_out, tm=1024):
    m = x.shape[0]
    row = pl.BlockSpec((tm, D_MODEL), lambda i: (i, 0))
    if slab_out:
        out_specs = [pl.BlockSpec((N_SLABS, tm, LANES), lambda i: (0, i, 0)), row]
        out_shape = [jax.ShapeDtypeStruct((N_SLABS, m, LANES), F32), jax.ShapeDtypeStruct((m, D_MODEL), BF16)]
    else:
        out_specs = row
        out_shape = jax.ShapeDtypeStruct((m, D_MODEL), F32)
    return pl.pallas_call(
        functools.partial(_ffn_kernel, slab_out=slab_out),
        grid=(m // tm,),
        in_specs=[row, _full(wu.shape), _full(wd.shape), _full((1, D_MODEL)), _full((1, D_MODEL))],
        out_specs=out_specs,
        out_shape=out_shape,
        compiler_params=_params(1),
        name="ffn_ln",
    )(x, wu, wd, g, b)


def _permute_kernel(x_ref, *out_refs, dils):
    for o_ref, d in zip(out_refs, dils):
        chunk = SPAN // d
        for r in range(d):
            for j in range(N_SLABS):
                o_ref[r * chunk:(r + 1) * chunk, j * LANES:(j + 1) * LANES] = (
                    x_ref[j, pl.ds(r, chunk, stride=d), :].astype(BF16))


def permute_tokens(x_slabs, dils):
    m = x_slabs.shape[1]
    return pl.pallas_call(
        functools.partial(_permute_kernel, dils=dils),
        grid=(m // SPAN,),
        in_specs=[pl.BlockSpec((N_SLABS, SPAN, LANES), lambda i: (0, i, 0))],
        out_specs=[pl.BlockSpec((SPAN, D_MODEL), lambda i: (i, 0)) for _ in dils],
        out_shape=[jax.ShapeDtypeStruct((m, D_MODEL), BF16) for _ in dils],
        compiler_params=_params(1),
        name="permute_tokens",
    )(x_slabs)


def _permute_table_kernel(cos_ref, sin_ref, *out_refs, dils):
    outs = iter(out_refs)
    for d in dils:
        chunk = SPAN // d
        for t_ref in (cos_ref, sin_ref):
            o_ref = next(outs)
            for r in range(d):
                o_ref[r * chunk:(r + 1) * chunk, :] = t_ref[pl.ds(r, chunk, stride=d), :]


def permute_tables(cos, sin_signed, dils):
    m = cos.shape[0]
    blk = pl.BlockSpec((SPAN, LANES), lambda i: (i, 0))
    outs = pl.pallas_call(
        functools.partial(_permute_table_kernel, dils=dils),
        grid=(m // SPAN,),
        in_specs=[blk, blk],
        out_specs=[blk] * (2 * len(dils)),
        out_shape=[jax.ShapeDtypeStruct((m, LANES), F32)] * (2 * len(dils)),
        compiler_params=_params(1),
        name="permute_tables",
    )(cos, sin_signed)
    return [(outs[2 * i], outs[2 * i + 1]) for i in range(len(dils))]


def _rope_slab(y, cos, sin_signed, first_half):
    partner = jnp.where(first_half, pltpu.roll(y, LANES - HEAD_DIM // 2, axis=1),
                        pltpu.roll(y, HEAD_DIM // 2, axis=1))
    return y * cos + partner * sin_signed


def _proj_kernel(x_ref, w_ref, wt_ref, cos_ref, sin_ref, *out_refs, segs, t_width):
    xb = x_ref[...]
    cos = cos_ref[...]
    sin_signed = sin_ref[...]
    lane = lax.broadcasted_iota(jnp.int32, cos.shape, 1)
    first_half = (lane % HEAD_DIM) < (HEAD_DIM // 2)
    col = 0
    for o_ref, (width, rope, scale) in zip(out_refs, segs):
        for c0 in range(0, width, MXU_COLS):
            cw = min(MXU_COLS, width - c0)
            y2 = jnp.dot(xb, w_ref[:, col + c0:col + c0 + cw], preferred_element_type=F32)
            for s0 in range(0, cw, LANES):
                y = y2[:, s0:s0 + LANES]
                if rope:
                    y = _rope_slab(y, cos, sin_signed, first_half)
                if scale != 1.0:
                    y = y * scale
                o_ref[:, c0 + s0:c0 + s0 + LANES] = y.astype(o_ref.dtype)
        col += width
    if t_width:
        t_ref = out_refs[len(segs)]
        for c0 in range(0, t_width, MXU_COLS):
            cw = min(MXU_COLS, t_width - c0)
            t_ref[c0:c0 + cw, :] = lax.dot_general(wt_ref[c0:c0 + cw, :], xb, NT,
                                                   preferred_element_type=F32).astype(t_ref.dtype)


def proj_rope(xb, w, wt, cos, sin_signed, segs, name, tm=1024):
    m = xb.shape[0]
    assert w.shape[1] == sum(s[0] for s in segs)
    t_width = 0 if wt is None else wt.shape[0]
    if wt is None:
        wt = jnp.zeros((8, D_MODEL), BF16)
    row = lambda wd: pl.BlockSpec((tm, wd), lambda i: (i, 0))
    out_specs = [row(s[0]) for s in segs]
    out_shape = [jax.ShapeDtypeStruct((m, s[0]), BF16) for s in segs]
    if t_width:
        out_specs.append(pl.BlockSpec((t_width, tm), lambda i: (0, i)))
        out_shape.append(jax.ShapeDtypeStruct((t_width, m), BF16))
    return pl.pallas_call(
        functools.partial(_proj_kernel, segs=segs, t_width=t_width),
        grid=(m // tm,),
        in_specs=[row(D_MODEL), _full(w.shape), _full(wt.shape), row(LANES), row(LANES)],
        out_specs=out_specs,
        out_shape=out_shape,
        compiler_params=_params(1),
        name=name,
    )(xb, w, wt, cos, sin_signed)


def _attn_kernel(*refs, n_heads, gqa, max_dist, dil, n_res, n_blocks, spans_per_batch, has_sink, want_lse):
    refs = list(refs)
    sink_ref = refs.pop(0) if has_sink else None
    n_halo = 1 if n_blocks == 1 else n_res
    q_ref, kc_ref, vc_ref = refs[:3]
    kh_refs, vh_refs = refs[3:3 + n_halo], refs[3 + n_halo:3 + 2 * n_halo]
    o_ref = refs[3 + 2 * n_halo]
    lse_ref = refs[4 + 2 * n_halo] if want_lse else None
    lse_scr, o_scr = refs[-2:]
    g = pl.program_id(0)
    r = pl.program_id(1)
    first_has_prev = (g % spans_per_batch) > 0
    key = lax.broadcasted_iota(jnp.int32, (2 * BLOCK, 2 * BLOCK), 0)
    qry = lax.broadcasted_iota(jnp.int32, (2 * BLOCK, 2 * BLOCK), 1) % BLOCK
    in_band_prev = jnp.logical_and(key < BLOCK, key >= qry + (BLOCK - max_dist))
    in_band_cur = jnp.logical_and(key >= BLOCK, key - BLOCK <= qry)
    bias_on = jnp.where(jnp.logical_or(in_band_prev, in_band_cur), 0.0, NEG_BIG)
    bias_first = jnp.where(jnp.logical_or(jnp.logical_and(in_band_prev, first_has_prev), in_band_cur),
                           0.0, NEG_BIG)
    low = lax.broadcasted_iota(jnp.int32, (BLOCK, LANES), 1) < HEAD_DIM
    lane2 = lax.broadcasted_iota(jnp.int32, (1, 2 * BLOCK), 1)
    ones_rows = jnp.ones((ONES_ROWS, 2 * BLOCK), BF16)
    n_pairs = n_heads // 2
    rep = n_heads // (kc_ref.shape[1] // HEAD_DIM)
    units = [(j, i, p) for j in range(n_res) for i in range(n_blocks) for p in range(n_pairs)]

    def rows_of(j, i):
        return slice((j * n_blocks + i) * BLOCK, (j * n_blocks + i + 1) * BLOCK)

    def prev_and_cur(j, i):
        return slice((j * n_blocks + i - 1) * BLOCK, (j * n_blocks + i + 1) * BLOCK)

    def halo_k(j, kcols):
        return kh_refs[0][j * BLOCK:(j + 1) * BLOCK, kcols] if n_halo == 1 else kh_refs[j][:, kcols]

    def halo_vt(j, vrows):
        return vh_refs[0][vrows, j * BLOCK:(j + 1) * BLOCK] if n_halo == 1 else vh_refs[j][vrows, :]

    def scores(u):
        j, i, p = units[u]
        rows = rows_of(j, i)
        qp = q_ref[rows, p * LANES:(p + 1) * LANES]
        zero = jnp.zeros_like(qp)
        if gqa:
            kv = (2 * p) // rep
            qr = pltpu.roll(qp, HEAD_DIM, axis=1)
            first, second = (qp, qr) if kv == 0 else (qr, qp)
            keep = low if kv == 0 else jnp.logical_not(low)
            rhs = jnp.concatenate([jnp.where(keep, first, zero), jnp.where(keep, second, zero)], axis=0)
            kcols = slice(0, LANES)
        else:
            rhs = jnp.concatenate([jnp.where(low, qp, zero), jnp.where(low, zero, qp)], axis=0)
            kcols = slice(p * LANES, (p + 1) * LANES)
        if i == 0:
            keys = jnp.concatenate([halo_k(j, kcols), kc_ref[rows, kcols]], axis=0)
            bias = bias_first
        else:
            keys, bias = kc_ref[prev_and_cur(j, i), kcols], bias_on
        return lax.dot_general(keys, rhs, NT, preferred_element_type=F32) + bias

    def softmax(u, s):
        p = units[u][2]
        m = jnp.max(s, axis=0, keepdims=True)
        sink_row = None
        if has_sink:
            sink_row = jnp.where(lane2 < BLOCK, sink_ref[2 * p], sink_ref[2 * p + 1]) * LOG2_E
            m = jnp.maximum(m, sink_row)
        return jnp.exp2(s - m).astype(BF16), m, sink_row

    def values(u, sm):
        j, i, p = units[u]
        vrows = (slice(((2 * p) // rep) * HEAD_DIM, ((2 * p) // rep + 1) * HEAD_DIM) if gqa
                 else slice(p * LANES, (p + 1) * LANES))
        if i == 0:
            vt = jnp.concatenate([halo_vt(j, vrows), vc_ref[vrows, rows_of(j, i)]], axis=1)
        else:
            vt = vc_ref[vrows, prev_and_cur(j, i)]
        return jnp.dot(jnp.concatenate([vt, ones_rows], axis=0), sm[0], preferred_element_type=F32)

    def store(u, res, sm):
        j, i, p = units[u]
        _, m, sink_row = sm
        n_feat = res.shape[0] - ONES_ROWS
        den = res[n_feat:n_feat + 1, :]
        if has_sink:
            den = den + jnp.exp2(sink_row - m)
        inv = 1.0 / den
        second = slice(0, HEAD_DIM) if gqa else slice(HEAD_DIM, 2 * HEAD_DIM)
        o_pair = jnp.concatenate([res[:HEAD_DIM, :BLOCK] * inv[:, :BLOCK],
                                  res[second, BLOCK:] * inv[:, BLOCK:]], axis=0)
        if dil == 1:
            dst = rows_of(j, i)
            o_ref[dst, p * LANES:(p + 1) * LANES] = o_pair.T.astype(o_ref.dtype)
        else:
            dst = pl.ds(r * n_res + j + dil * BLOCK * i, BLOCK, stride=dil)
            o_scr[p, dst, :] = o_pair.T
        if want_lse:
            lse = (m + jnp.log2(den)) * LN_2
            if p == 0:
                lse_scr[...] = jnp.zeros_like(lse_scr)
            lse_scr[2 * p:2 * p + 1, :] = lse[:, :BLOCK]
            lse_scr[2 * p + 1:2 * p + 2, :] = lse[:, BLOCK:]
            if p == n_pairs - 1:
                lse_ref[dst, :] = lse_scr[...].T

    _software_pipeline(len(units), scores, softmax, values, store, lags=ATTN_LAGS)

    if dil > 1:
        @pl.when(r == pl.num_programs(1) - 1)
        def _():
            for p in range(n_pairs):
                o_ref[:, p * LANES:(p + 1) * LANES] = o_scr[p].astype(o_ref.dtype)


def banded_attention(q, k, vt, max_dist, dil, seq, sink=None, want_lse=True):
    m, qw = q.shape
    kw = k.shape[1]
    n_heads = qw // HEAD_DIM
    gqa = kw != qw
    span = SPAN if dil > 1 else PLAIN_SPAN
    chunk = span // dil
    n_blocks = chunk // BLOCK
    n_res = max(ATTN_UNITS_BLOCKS // n_blocks, 1)
    spans_per_batch = seq // span
    steps = dil // n_res
    rows = chunk * n_res

    def cur(g, r):
        return (g * steps + r, 0)

    def cur_t(g, r):
        return (0, g * steps + r)

    def halo_block(g, r, j):
        return jnp.maximum(((g - 1) * dil + r * n_res + j + 1) * n_blocks - 1, 0)

    if n_blocks == 1:
        halo = BLOCK * n_res
        k_halos = [pl.BlockSpec((halo, kw), lambda g, r: (halo_block(g, r, 0) // n_res, 0))]
        v_halos = [pl.BlockSpec((kw, halo), lambda g, r: (0, halo_block(g, r, 0) // n_res))]
    else:
        k_halos = [pl.BlockSpec((BLOCK, kw), lambda g, r, j=j: (halo_block(g, r, j), 0)) for j in range(n_res)]
        v_halos = [pl.BlockSpec((kw, BLOCK), lambda g, r, j=j: (0, halo_block(g, r, j))) for j in range(n_res)]
    in_specs = [pl.BlockSpec((rows, qw), cur), pl.BlockSpec((rows, kw), cur), pl.BlockSpec((kw, rows), cur_t),
                *k_halos, *v_halos]
    args = [q, k, vt] + [k] * len(k_halos) + [vt] * len(v_halos)
    if sink is not None:
        in_specs = [pl.BlockSpec(memory_space=pltpu.SMEM)] + in_specs
        args = [sink] + args
    out_specs = [pl.BlockSpec((span, qw), lambda g, r: (g, 0))]
    out_shape = [jax.ShapeDtypeStruct((m, qw), BF16)]
    o_scr_rows = span if dil > 1 else 8
    if want_lse:
        out_specs.append(pl.BlockSpec((span, LANES), lambda g, r: (g, 0)))
        out_shape.append(jax.ShapeDtypeStruct((m, LANES), F32))
    return pl.pallas_call(
        functools.partial(_attn_kernel, n_heads=n_heads, gqa=gqa, max_dist=max_dist, dil=dil, n_res=n_res,
                          n_blocks=n_blocks, spans_per_batch=spans_per_batch,
                          has_sink=sink is not None, want_lse=want_lse),
        grid=(m // span, steps),
        in_specs=in_specs,
        out_specs=out_specs,
        out_shape=out_shape,
        scratch_shapes=[pltpu.VMEM((LANES, BLOCK), F32), pltpu.VMEM((qw // LANES, o_scr_rows, LANES), F32)],
        compiler_params=_params(2),
        name="banded_attention",
    )(*args)


def _retention_kernel(q_ref, k_ref, v_ref, gate_ref, o_ref, state_ref, dec_ref):
    n = pl.program_id(0)
    n_pairs = C_HEADS // 2
    log_g = [math.log1p(-2.0 ** (-5.0 - h)) for h in range(C_HEADS)]
    row = lax.broadcasted_iota(jnp.int32, (C_CHUNK, LANES), 0)
    lane = lax.broadcasted_iota(jnp.int32, (C_CHUNK, LANES), 1)
    low = lane < C_KEY_DIM

    @pl.when(n == 0)
    def _():
        state_ref[...] = jnp.zeros_like(state_ref)
        rel = (row - lane).astype(F32)
        idx = row.astype(F32)
        for h in range(C_HEADS):
            dec_ref[0, h] = jnp.where(rel >= 0, jnp.exp(log_g[h] * jnp.maximum(rel, 0.0)), 0.0)
            dec_ref[1, h] = jnp.exp(log_g[h] * (idx + 1.0))
        for p in range(n_pairs):
            lg = jnp.where(low, log_g[2 * p], log_g[2 * p + 1])
            dec_ref[2, p] = jnp.exp(lg * (C_CHUNK - 1.0 - idx))

    units = [(b, p) for b in range(q_ref.shape[0]) for p in range(n_pairs)]

    def scores(u):
        b, p = units[u]
        cols = slice(p * LANES, (p + 1) * LANES)
        qp, kp = q_ref[b, :, cols], k_ref[b, :, cols]
        zero = jnp.zeros_like(qp)
        lhs = jnp.concatenate([jnp.where(low, qp, zero), jnp.where(low, zero, qp)], axis=0)
        s = lax.dot_general(lhs, kp, NT, preferred_element_type=F32)
        qs = jnp.dot(lhs, state_ref[b, p].astype(BF16), preferred_element_type=F32)
        return s, qs, kp

    def decay(u, sc):
        _, p = units[u]
        s, qs, kp = sc
        a = [(s[hh * C_CHUNK:(hh + 1) * C_CHUNK] * dec_ref[0, 2 * p + hh]).astype(BF16) for hh in range(2)]
        kd_t = (kp.astype(F32) * dec_ref[2, p]).T.astype(BF16)
        return a, kd_t, qs

    def values(u, dc):
        b, p = units[u]
        a, kd_t, qs = dc
        outs = []
        for hh in range(2):
            h = 2 * p + hh
            v = v_ref[b, :, h * C_VAL_DIM:(h + 1) * C_VAL_DIM]
            srows = slice(hh * C_KEY_DIM, (hh + 1) * C_KEY_DIM)
            res = jnp.dot(jnp.concatenate([a[hh], kd_t[srows, :]], axis=0), v, preferred_element_type=F32)
            outs.append(res[:C_CHUNK] + qs[hh * C_CHUNK:(hh + 1) * C_CHUNK] * dec_ref[1, h])
            state_ref[b, p, srows, :] = (state_ref[b, p, srows, :] * math.exp(log_g[h] * C_CHUNK)
                                         + res[C_CHUNK:])
        return outs

    def store(u, outs):
        b, p = units[u]
        for hh in range(2):
            h = 2 * p + hh
            vs = slice(h * C_VAL_DIM, (h + 1) * C_VAL_DIM)
            o = outs[hh]
            mu = jnp.mean(o, axis=-1, keepdims=True)
            oc = o - mu
            var = jnp.mean(oc * oc, axis=-1, keepdims=True)
            gate = gate_ref[b, :, vs].astype(F32)
            o_ref[b, :, vs] = (gate * jax.nn.sigmoid(gate) * (oc * lax.rsqrt(var + LN_EPS))).astype(o_ref.dtype)

    _software_pipeline(len(units), scores, decay, values, lambda u, outs, _: store(u, outs), lags=RET_LAGS)


def retention_gated(q, k, v, gate):
    bsz, seq, _ = q.shape
    blk = lambda w: pl.BlockSpec((bsz, C_CHUNK, w), lambda n: (0, n, 0))
    return pl.pallas_call(
        _retention_kernel,
        grid=(seq // C_CHUNK,),
        in_specs=[blk(C_QK), blk(C_QK), blk(C_V), blk(C_V)],
        out_specs=blk(C_V),
        out_shape=jax.ShapeDtypeStruct((bsz, seq, C_V), BF16),
        scratch_shapes=[pltpu.VMEM((bsz, C_HEADS // 2, 2 * C_KEY_DIM, C_VAL_DIM), F32),
                        pltpu.VMEM((3, C_HEADS, C_CHUNK, LANES), F32)],
        compiler_params=_params(1),
        name="retention",
    )(q, k, v, gate)


def _merge_kernel(x_ref, xb_ref, o1_ref, o2_ref, o3_ref, l1_ref, l2_ref, l3_ref, yb_ref, yc_ref,
                  wg_ref, gb_ref, pa_ref, pb_ref, pc_ref, wo_ref, e_ref, g_ref, b_ref, out_ref):
    expand = e_ref[...]
    lane = lax.broadcasted_iota(jnp.int32, (MERGE_SUB, LANES), 1)

    def per_lane(w):
        w = jnp.where(lane < A_HEADS, w, 0.0)
        hi = w.astype(BF16).astype(F32)
        packed = hi + pltpu.roll(w - hi, LANES // 2, axis=1)
        return jnp.dot(packed.astype(BF16), expand, preferred_element_type=F32)

    for sub in range(xb_ref.shape[0] // MERGE_SUB):
        rows = slice(sub * MERGE_SUB, (sub + 1) * MERGE_SUB)
        x = jnp.concatenate([x_ref[j, rows, :] for j in range(N_SLABS)], axis=1)
        xb = xb_ref[rows, :]
        l1, l2, l3 = l1_ref[rows, :], l2_ref[rows, :], l3_ref[rows, :]
        m = jnp.maximum(jnp.maximum(l1, l2), l3)
        e1, e2, e3 = jnp.exp(l1 - m), jnp.exp(l2 - m), jnp.exp(l3 - m)
        inv = 1.0 / (e1 + e2 + e3)
        o3 = o3_ref[rows, :].astype(F32)
        ya = (o3 + per_lane(e1 * inv) * (o1_ref[rows, :].astype(F32) - o3)
              + per_lane(e2 * inv) * (o2_ref[rows, :].astype(F32) - o3))

        def gate(j, xb=xb):
            cols = slice(j * D_MODEL, (j + 1) * D_MODEL)
            return jax.nn.sigmoid(jnp.dot(xb, wg_ref[:, cols], preferred_element_type=F32) + gb_ref[:, cols])

        merged = gate(0) * jnp.dot(ya.astype(BF16), pa_ref[...], preferred_element_type=F32)
        merged += gate(1) * jnp.dot(yb_ref[rows, :], pb_ref[...], preferred_element_type=F32)
        merged += gate(2) * jnp.dot(yc_ref[rows, :], pc_ref[...], preferred_element_type=F32)
        mix = jnp.dot(merged.astype(BF16), wo_ref[...], preferred_element_type=F32)
        out_ref[rows, :] = _layer_norm(ALPHA * x + mix, g_ref[...], b_ref[...])


def merge_ln(x_slabs, xb, o1, o2, o3, l1, l2, l3, yb, yc, wg, gb, pa, pb, pc, wo, g, b, tm=512):
    m = xb.shape[0]
    head = jnp.arange(A_W) // HEAD_DIM
    expand = ((jnp.arange(LANES)[:, None] % (LANES // 2)) == head[None, :]).astype(BF16)
    row = lambda w: pl.BlockSpec((tm, w), lambda i: (i, 0))
    return pl.pallas_call(
        _merge_kernel,
        grid=(m // tm,),
        in_specs=[pl.BlockSpec((N_SLABS, tm, LANES), lambda i: (0, i, 0)), row(D_MODEL),
                  row(A_W), row(A_W), row(A_W), row(LANES), row(LANES), row(LANES), row(B_QW), row(C_V),
                  _full(wg.shape), _full(gb.shape), _full(pa.shape),
                  _full(pb.shape), _full(pc.shape), _full(wo.shape), _full(expand.shape),
                  _full((1, D_MODEL)), _full((1, D_MODEL))],
        out_specs=row(D_MODEL),
        out_shape=jax.ShapeDtypeStruct((m, D_MODEL), F32),
        compiler_params=_params(1),
        name="merge_ln",
    )(x_slabs, xb, o1, o2, o3, l1, l2, l3, yb, yc, wg, gb, pa, pb, pc, wo, expand, g, b)


def kernel(x, positions, w_in, gate_bias, attn_sinks, w_proj_a, w_proj_b, w_proj_c, w_out, ffn1_up, ffn1_down, ffn2_up, ffn2_down, ln1_g, ln1_b, ln2_g, ln2_b, ln3_g, ln3_b):
    bsz, seq, _ = x.shape
    m = bsz * seq
    dils = tuple(d for _, d in A_GROUPS)
    tables = {1: rope_tables(positions.reshape(m, 1))}
    tables.update(zip(dils[1:], permute_tables(*tables[1], dils[1:])))
    x = x.reshape(m, D_MODEL)
    a_end = 9 * A_W
    b_end = a_end + B_QW + 2 * B_KVW
    c_end = b_end + 2 * C_QK + 2 * C_V
    qk_scale = HEAD_DIM ** -0.5 * LOG2_E
    seg_a = ((A_W, True, qk_scale), (A_W, True, 1.0))
    seg_b = ((B_QW, True, qk_scale), (B_KVW, True, 1.0))
    seg_c = ((C_QK, True, 1.0), (C_QK, True, C_KEY_DIM ** -0.5), (C_V, False, 1.0), (C_V, False, 1.0))
    row = lambda t: t.reshape(1, -1)
    for l in range(DEPTH):
        x_slabs, xb = ffn_ln(x, ffn1_up[l].astype(BF16), ffn1_down[l].astype(BF16), row(ln1_g[l]), row(ln1_b[l]), True)
        xb_by_dil = dict(zip(dils[1:], permute_tokens(x_slabs, dils[1:])))
        xb_by_dil[1] = xb
        w = lambda lo, hi: w_in[l, :, lo:hi].astype(BF16)
        outs, lses = [], []
        for gi, (window, dil) in enumerate(A_GROUPS):
            c0 = 3 * gi * A_W
            q, k, vt = proj_rope(xb_by_dil[dil], w(c0, c0 + 2 * A_W), w(c0 + 2 * A_W, c0 + 3 * A_W).T,
                                 *tables[dil], seg_a, "proj_a")
            o, lse = banded_attention(q, k, vt, window // dil, dil, seq)
            outs.append(o)
            lses.append(lse)
        qb, kb, vtb = proj_rope(xb, w(a_end, a_end + B_QW + B_KVW), w(a_end + B_QW + B_KVW, b_end).T,
                                *tables[1], seg_b, "proj_b")
        yb, = banded_attention(qb, kb, vtb, B_WINDOW - 1, 1, seq, attn_sinks[l], want_lse=False)
        qc, kc, vc, gc = proj_rope(xb, w(b_end, c_end), None, *tables[1], seg_c, "proj_c")
        yc = retention_gated(qc.reshape(bsz, seq, C_QK), kc.reshape(bsz, seq, C_QK),
                             vc.reshape(bsz, seq, C_V), gc.reshape(bsz, seq, C_V))
        x = merge_ln(x_slabs, xb, *outs, *lses, yb, yc.reshape(m, C_V),
                     w(c_end, c_end + 3 * D_MODEL), row(gate_bias[l]), w_proj_a[l].astype(BF16),
                     w_proj_b[l].astype(BF16), w_proj_c[l].astype(BF16), w_out[l].astype(BF16),
                     row(ln2_g[l]), row(ln2_b[l]))
        x = ffn_ln(x, ffn2_up[l].astype(BF16), ffn2_down[l].astype(BF16), row(ln3_g[l]), row(ln3_b[l]), False)
    return x.reshape(bsz, seq, D_MODEL)
```

```python
import functools
import math

import jax
import jax.numpy as jnp
from jax import lax
from jax.experimental import pallas as pl
from jax.experimental.pallas import tpu as pltpu

D_MODEL = 1024
DEPTH = 2
HEAD_DIM = 64
BLOCK = 128
A_GROUPS = ((128, 1), (512, 4), (2048, 16))
A_HEADS = 12
A_W = A_HEADS * HEAD_DIM
B_Q_HEADS = 16
B_KV_HEADS = 2
B_WINDOW = 128
B_QW = B_Q_HEADS * HEAD_DIM
B_KVW = B_KV_HEADS * HEAD_DIM
C_HEADS = 8
C_KEY_DIM = 64
C_VAL_DIM = 128
C_CHUNK = 128
C_QK = C_HEADS * C_KEY_DIM
C_V = C_HEADS * C_VAL_DIM
D_FF = 2816
ROPE_THETA = 10000.0
LN_EPS = 1e-5
ALPHA = (2.0 * DEPTH) ** 0.25

LANES = 128
MXU_COLS = 256
FF_CHUNK = 256
FFN_SUB = 512
MERGE_SUB = 256
SPAN = 2048
PLAIN_SPAN = 1024
ATTN_UNITS_BLOCKS = 8
ATTN_LAGS = (3, 4, 7)
RET_LAGS = (1, 2, 3)
NEG_BIG = -1e30
ONES_ROWS = 16
LOG2_E = math.log2(math.e)
LN_2 = math.log(2.0)
VMEM_LIMIT = 56 * 1024 * 1024
N_SLABS = D_MODEL // LANES

F32 = jnp.float32
BF16 = jnp.bfloat16
NT = (((1,), (1,)), ((), ()))


def _params(n_axes):
    return pltpu.CompilerParams(dimension_semantics=("arbitrary",) * n_axes,
                                vmem_limit_bytes=VMEM_LIMIT)


def _layer_norm(y, g, b):
    mu = jnp.mean(y, axis=-1, keepdims=True)
    yc = y - mu
    var = jnp.mean(yc * yc, axis=-1, keepdims=True)
    return yc * lax.rsqrt(var + LN_EPS) * g + b


def _full(shape):
    nd = len(shape)
    return pl.BlockSpec(shape, lambda *_: (0,) * nd, pipeline_mode=pl.Buffered(1))


def _software_pipeline(n, stage_a, stage_b, stage_c, stage_d, lags=(1, 2, 3)):
    a_val, b_val, c_val = {}, {}, {}
    lag_b, lag_c, lag_d = lags
    for t in range(n + lag_d):
        if t < n:
            a_val[t] = stage_a(t)
        if 0 <= t - lag_b < n:
            b_val[t - lag_b] = stage_b(t - lag_b, a_val.pop(t - lag_b))
        if 0 <= t - lag_c < n:
            c_val[t - lag_c] = stage_c(t - lag_c, b_val[t - lag_c])
        if 0 <= t - lag_d < n:
            stage_d(t - lag_d, c_val.pop(t - lag_d), b_val.pop(t - lag_d))


def _rope_table_kernel(pos_ref, inv_ref, sign_ref, cos_ref, sin_ref):
    ang = pos_ref[...].astype(F32) * inv_ref[...]
    cos_ref[...] = jnp.cos(ang)
    sin_ref[...] = jnp.sin(ang) * sign_ref[...]


def rope_tables(pos_col, tm=1024):
    m = pos_col.shape[0]
    half = HEAD_DIM // 2
    inv = ROPE_THETA ** (-jnp.arange(half, dtype=F32) / half)
    inv_row = jnp.tile(inv, LANES // half)[None, :]
    lane = jnp.arange(LANES)
    sign_row = jnp.where(lane % HEAD_DIM < half, -1.0, 1.0).astype(F32)[None, :]
    return pl.pallas_call(
        _rope_table_kernel,
        grid=(m // tm,),
        in_specs=[pl.BlockSpec((tm, 1), lambda i: (i, 0)), _full((1, LANES)), _full((1, LANES))],
        out_specs=[pl.BlockSpec((tm, LANES), lambda i: (i, 0))] * 2,
        out_shape=[jax.ShapeDtypeStruct((m, LANES), F32)] * 2,
        compiler_params=_params(1),
        name="rope_table",
    )(pos_col, inv_row, sign_row)


def _ffn_kernel(x_ref, wu_ref, wd_ref, g_ref, b_ref, *out_refs, slab_out):
    n_chunks = D_FF // FF_CHUNK
    for sub in range(x_ref.shape[0] // FFN_SUB):
        rows = slice(sub * FFN_SUB, (sub + 1) * FFN_SUB)
        x = x_ref[rows, :]
        xb = x.astype(BF16)

        def hidden(c, xb=xb):
            lo = c * FF_CHUNK
            a = jnp.dot(xb, wu_ref[:, lo:lo + FF_CHUNK], preferred_element_type=F32)
            b = jnp.dot(xb, wu_ref[:, D_FF + lo:D_FF + lo + FF_CHUNK], preferred_element_type=F32)
            return (a * jax.nn.sigmoid(a) * b).astype(BF16)

        h = hidden(0)
        acc = None
        for c in range(n_chunks):
            h_next = hidden(c + 1) if c + 1 < n_chunks else None
            d = jnp.dot(h, wd_ref[c * FF_CHUNK:(c + 1) * FF_CHUNK, :], preferred_element_type=F32)
            acc = d if acc is None else acc + d
            h = h_next
        y = _layer_norm(ALPHA * x + 0.5 * acc, g_ref[...], b_ref[...])
        if slab_out:
            slab_ref, bf_ref = out_refs
            for j in range(N_SLABS):
                slab_ref[j, rows, :] = y[:, j * LANES:(j + 1) * LANES]
            bf_ref[rows, :] = y.astype(BF16)
        else:
            out_refs[0][rows, :] = y


def ffn_ln(x, wu, wd, g, b, slab_out, tm=1024):
    m = x.shape[0]
    row = pl.BlockSpec((tm, D_MODEL), lambda i: (i, 0))
    if slab_out:
        out_specs = [pl.BlockSpec((N_SLABS, tm, LANES), lambda i: (0, i, 0)), row]
        out_shape = [jax.ShapeDtypeStruct((N_SLABS, m, LANES), F32), jax.ShapeDtypeStruct((m, D_MODEL), BF16)]
    else:
        out_specs = row
        out_shape = jax.ShapeDtypeStruct((m, D_MODEL), F32)
    return pl.pallas_call(
        functools.partial(_ffn_kernel, slab_out=slab_out),
        grid=(m // tm,),
        in_specs=[row, _full(wu.shape), _full(wd.shape), _full((1, D_MODEL)), _full((1, D_MODEL))],
        out_specs=out_specs,
        out_shape=out_shape,
        compiler_params=_params(1),
        name="ffn_ln",
    )(x, wu, wd, g, b)


def _permute_kernel(x_ref, *out_refs, dils):
    for o_ref, d in zip(out_refs, dils):
        chunk = SPAN // d
        for r in range(d):
            for j in range(N_SLABS):
                o_ref[r * chunk:(r + 1) * chunk, j * LANES:(j + 1) * LANES] = (
                    x_ref[j, pl.ds(r, chunk, stride=d), :].astype(BF16))


def permute_tokens(x_slabs, dils):
    m = x_slabs.shape[1]
    return pl.pallas_call(
        functools.partial(_permute_kernel, dils=dils),
        grid=(m // SPAN,),
        in_specs=[pl.BlockSpec((N_SLABS, SPAN, LANES), lambda i: (0, i, 0))],
        out_specs=[pl.BlockSpec((SPAN, D_MODEL), lambda i: (i, 0)) for _ in dils],
        out_shape=[jax.ShapeDtypeStruct((m, D_MODEL), BF16) for _ in dils],
        compiler_params=_params(1),
        name="permute_tokens",
    )(x_slabs)


def _permute_table_kernel(cos_ref, sin_ref, *out_refs, dils):
    outs = iter(out_refs)
    for d in dils:
        chunk = SPAN // d
        for t_ref in (cos_ref, sin_ref):
            o_ref = next(outs)
            for r in range(d):
                o_ref[r * chunk:(r + 1) * chunk, :] = t_ref[pl.ds(r, chunk, stride=d), :]


def permute_tables(cos, sin_signed, dils):
    m = cos.shape[0]
    blk = pl.BlockSpec((SPAN, LANES), lambda i: (i, 0))
    outs = pl.pallas_call(
        functools.partial(_permute_table_kernel, dils=dils),
        grid=(m // SPAN,),
        in_specs=[blk, blk],
        out_specs=[blk] * (2 * len(dils)),
        out_shape=[jax.ShapeDtypeStruct((m, LANES), F32)] * (2 * len(dils)),
        compiler_params=_params(1),
        name="permute_tables",
    )(cos, sin_signed)
    return [(outs[2 * i], outs[2 * i + 1]) for i in range(len(dils))]


def _rope_slab(y, cos, sin_signed, first_half):
    partner = jnp.where(first_half, pltpu.roll(y, LANES - HEAD_DIM // 2, axis=1),
                        pltpu.roll(y, HEAD_DIM // 2, axis=1))
    return y * cos + partner * sin_signed


def _proj_kernel(*refs, n_w, plan, segs, t_width):
    x_ref, w_refs = refs[0], refs[1:1 + n_w]
    cos_ref, sin_ref = refs[1 + n_w:3 + n_w]
    out_refs = refs[3 + n_w:-2]
    w_scr, wt_scr = refs[-2:]

    @pl.when(pl.program_id(0) == 0)
    def _():
        for idx, lo, hi, transposed, dst in plan:
            blk = w_refs[idx][:, lo:hi]
            if transposed:
                wt_scr[dst:dst + hi - lo, :] = blk.T.astype(BF16)
            else:
                w_scr[:, dst:dst + hi - lo] = blk.astype(BF16)

    xb = x_ref[...]
    cos = cos_ref[...]
    sin_signed = sin_ref[...]
    lane = lax.broadcasted_iota(jnp.int32, cos.shape, 1)
    first_half = (lane % HEAD_DIM) < (HEAD_DIM // 2)
    col = 0
    for o_ref, (width, rope, scale) in zip(out_refs, segs):
        for c0 in range(0, width, MXU_COLS):
            cw = min(MXU_COLS, width - c0)
            y2 = jnp.dot(xb, w_scr[:, col + c0:col + c0 + cw], preferred_element_type=F32)
            for s0 in range(0, cw, LANES):
                y = y2[:, s0:s0 + LANES]
                if rope:
                    y = _rope_slab(y, cos, sin_signed, first_half)
                if scale != 1.0:
                    y = y * scale
                o_ref[:, c0 + s0:c0 + s0 + LANES] = y.astype(o_ref.dtype)
        col += width
    if t_width:
        t_ref = out_refs[len(segs)]
        for c0 in range(0, t_width, MXU_COLS):
            cw = min(MXU_COLS, t_width - c0)
            t_ref[c0:c0 + cw, :] = lax.dot_general(wt_scr[c0:c0 + cw, :], xb, NT,
                                                   preferred_element_type=F32).astype(t_ref.dtype)


def proj_rope(xb, w_in, layer, col0, block_w, t_width, cos, sin_signed, segs, name, tm=1024):
    m = xb.shape[0]
    n_row = sum(s[0] for s in segs)
    total = n_row + t_width
    assert col0 % block_w == 0 and total % block_w == 0
    n_w = total // block_w
    plan = []
    for idx in range(n_w):
        lo, hi = idx * block_w, (idx + 1) * block_w
        if lo < n_row:
            plan.append((idx, 0, min(hi, n_row) - lo, False, lo))
        if hi > n_row:
            start = max(lo, n_row)
            plan.append((idx, start - lo, block_w, True, start - n_row))
    w_specs = [pl.BlockSpec((None, D_MODEL, block_w), lambda i, cb=col0 // block_w + idx: (layer, 0, cb),
                            pipeline_mode=pl.Buffered(1)) for idx in range(n_w)]
    row = lambda wd: pl.BlockSpec((tm, wd), lambda i: (i, 0))
    out_specs = [row(s[0]) for s in segs]
    out_shape = [jax.ShapeDtypeStruct((m, s[0]), BF16) for s in segs]
    if t_width:
        out_specs.append(pl.BlockSpec((t_width, tm), lambda i: (0, i)))
        out_shape.append(jax.ShapeDtypeStruct((t_width, m), BF16))
    return pl.pallas_call(
        functools.partial(_proj_kernel, n_w=n_w, plan=tuple(plan), segs=segs, t_width=t_width),
        grid=(m // tm,),
        in_specs=[row(D_MODEL), *w_specs, row(LANES), row(LANES)],
        out_specs=out_specs,
        out_shape=out_shape,
        scratch_shapes=[pltpu.VMEM((D_MODEL, n_row), BF16), pltpu.VMEM((max(t_width, 16), D_MODEL), BF16)],
        compiler_params=_params(1),
        name=name,
    )(xb, *([w_in] * n_w), cos, sin_signed)


def _attn_kernel(*refs, n_heads, gqa, max_dist, dil, n_res, n_blocks, spans_per_batch, has_sink, want_lse):
    refs = list(refs)
    sink_ref = refs.pop(0) if has_sink else None
    n_halo = 1 if n_blocks == 1 else n_res
    q_ref, kc_ref, vc_ref = refs[:3]
    kh_refs, vh_refs = refs[3:3 + n_halo], refs[3 + n_halo:3 + 2 * n_halo]
    o_ref = refs[3 + 2 * n_halo]
    lse_ref = refs[4 + 2 * n_halo] if want_lse else None
    lse_scr, o_scr = refs[-2:]
    g = pl.program_id(0)
    r = pl.program_id(1)
    first_has_prev = (g % spans_per_batch) > 0
    key = lax.broadcasted_iota(jnp.int32, (2 * BLOCK, 2 * BLOCK), 0)
    qry = lax.broadcasted_iota(jnp.int32, (2 * BLOCK, 2 * BLOCK), 1) % BLOCK
    in_band_prev = jnp.logical_and(key < BLOCK, key >= qry + (BLOCK - max_dist))
    in_band_cur = jnp.logical_and(key >= BLOCK, key - BLOCK <= qry)
    bias_on = jnp.where(jnp.logical_or(in_band_prev, in_band_cur), 0.0, NEG_BIG)
    bias_first = jnp.where(jnp.logical_or(jnp.logical_and(in_band_prev, first_has_prev), in_band_cur),
                           0.0, NEG_BIG)
    low = lax.broadcasted_iota(jnp.int32, (BLOCK, LANES), 1) < HEAD_DIM
    lane2 = lax.broadcasted_iota(jnp.int32, (1, 2 * BLOCK), 1)
    ones_rows = jnp.ones((ONES_ROWS, 2 * BLOCK), BF16)
    n_pairs = n_heads // 2
    rep = n_heads // (kc_ref.shape[1] // HEAD_DIM)
    units = [(j, i, p) for j in range(n_res) for i in range(n_blocks) for p in range(n_pairs)]

    def rows_of(j, i):
        return slice((j * n_blocks + i) * BLOCK, (j * n_blocks + i + 1) * BLOCK)

    def prev_and_cur(j, i):
        return slice((j * n_blocks + i - 1) * BLOCK, (j * n_blocks + i + 1) * BLOCK)

    def halo_k(j, kcols):
        return kh_refs[0][j * BLOCK:(j + 1) * BLOCK, kcols] if n_halo == 1 else kh_refs[j][:, kcols]

    def halo_vt(j, vrows):
        return vh_refs[0][vrows, j * BLOCK:(j + 1) * BLOCK] if n_halo == 1 else vh_refs[j][vrows, :]

    def scores(u):
        j, i, p = units[u]
        rows = rows_of(j, i)
        qp = q_ref[rows, p * LANES:(p + 1) * LANES]
        zero = jnp.zeros_like(qp)
        if gqa:
            kv = (2 * p) // rep
            qr = pltpu.roll(qp, HEAD_DIM, axis=1)
            first, second = (qp, qr) if kv == 0 else (qr, qp)
            keep = low if kv == 0 else jnp.logical_not(low)
            rhs = jnp.concatenate([jnp.where(keep, first, zero), jnp.where(keep, second, zero)], axis=0)
            kcols = slice(0, LANES)
        else:
            rhs = jnp.concatenate([jnp.where(low, qp, zero), jnp.where(low, zero, qp)], axis=0)
            kcols = slice(p * LANES, (p + 1) * LANES)
        if i == 0:
            keys = jnp.concatenate([halo_k(j, kcols), kc_ref[rows, kcols]], axis=0)
            bias = bias_first
        else:
            keys, bias = kc_ref[prev_and_cur(j, i), kcols], bias_on
        return lax.dot_general(keys, rhs, NT, preferred_element_type=F32) + bias

    def softmax(u, s):
        p = units[u][2]
        m = jnp.max(s, axis=0, keepdims=True)
        sink_row = None
        if has_sink:
            sink_row = jnp.where(lane2 < BLOCK, sink_ref[2 * p], sink_ref[2 * p + 1]) * LOG2_E
            m = jnp.maximum(m, sink_row)
        return jnp.exp2(s - m).astype(BF16), m, sink_row

    def values(u, sm):
        j, i, p = units[u]
        vrows = (slice(((2 * p) // rep) * HEAD_DIM, ((2 * p) // rep + 1) * HEAD_DIM) if gqa
                 else slice(p * LANES, (p + 1) * LANES))
        if i == 0:
            vt = jnp.concatenate([halo_vt(j, vrows), vc_ref[vrows, rows_of(j, i)]], axis=1)
        else:
            vt = vc_ref[vrows, prev_and_cur(j, i)]
        return jnp.dot(jnp.concatenate([vt, ones_rows], axis=0), sm[0], preferred_element_type=F32)

    def store(u, res, sm):
        j, i, p = units[u]
        _, m, sink_row = sm
        n_feat = res.shape[0] - ONES_ROWS
        den = res[n_feat:n_feat + 1, :]
        if has_sink:
            den = den + jnp.exp2(sink_row - m)
        inv = 1.0 / den
        second = slice(0, HEAD_DIM) if gqa else slice(HEAD_DIM, 2 * HEAD_DIM)
        o_pair = jnp.concatenate([res[:HEAD_DIM, :BLOCK] * inv[:, :BLOCK],
                                  res[second, BLOCK:] * inv[:, BLOCK:]], axis=0)
        if dil == 1:
            dst = rows_of(j, i)
            o_ref[dst, p * LANES:(p + 1) * LANES] = o_pair.T.astype(o_ref.dtype)
        else:
            dst = pl.ds(r * n_res + j + dil * BLOCK * i, BLOCK, stride=dil)
            o_scr[p, dst, :] = o_pair.T
        if want_lse:
            lse = (m + jnp.log2(den)) * LN_2
            if p == 0:
                lse_scr[...] = jnp.zeros_like(lse_scr)
            lse_scr[2 * p:2 * p + 1, :] = lse[:, :BLOCK]
            lse_scr[2 * p + 1:2 * p + 2, :] = lse[:, BLOCK:]
            if p == n_pairs - 1:
                lse_ref[dst, :] = lse_scr[...].T

    _software_pipeline(len(units), scores, softmax, values, store, lags=ATTN_LAGS)

    if dil > 1:
        @pl.when(r == pl.num_programs(1) - 1)
        def _():
            for p in range(n_pairs):
                o_ref[:, p * LANES:(p + 1) * LANES] = o_scr[p].astype(o_ref.dtype)


def banded_attention(q, k, vt, max_dist, dil, seq, sink=None, want_lse=True):
    m, qw = q.shape
    kw = k.shape[1]
    n_heads = qw // HEAD_DIM
    gqa = kw != qw
    span = SPAN if dil > 1 else PLAIN_SPAN
    chunk = span // dil
    n_blocks = chunk // BLOCK
    n_res = max(ATTN_UNITS_BLOCKS // n_blocks, 1)
    spans_per_batch = seq // span
    steps = dil // n_res
    rows = chunk * n_res

    def cur(g, r):
        return (g * steps + r, 0)

    def cur_t(g, r):
        return (0, g * steps + r)

    def halo_block(g, r, j):
        return jnp.maximum(((g - 1) * dil + r * n_res + j + 1) * n_blocks - 1, 0)

    if n_blocks == 1:
        halo = BLOCK * n_res
        k_halos = [pl.BlockSpec((halo, kw), lambda g, r: (halo_block(g, r, 0) // n_res, 0))]
        v_halos = [pl.BlockSpec((kw, halo), lambda g, r: (0, halo_block(g, r, 0) // n_res))]
    else:
        k_halos = [pl.BlockSpec((BLOCK, kw), lambda g, r, j=j: (halo_block(g, r, j), 0)) for j in range(n_res)]
        v_halos = [pl.BlockSpec((kw, BLOCK), lambda g, r, j=j: (0, halo_block(g, r, j))) for j in range(n_res)]
    in_specs = [pl.BlockSpec((rows, qw), cur), pl.BlockSpec((rows, kw), cur), pl.BlockSpec((kw, rows), cur_t),
                *k_halos, *v_halos]
    args = [q, k, vt] + [k] * len(k_halos) + [vt] * len(v_halos)
    if sink is not None:
        in_specs = [pl.BlockSpec(memory_space=pltpu.SMEM)] + in_specs
        args = [sink] + args
    out_specs = [pl.BlockSpec((span, qw), lambda g, r: (g, 0))]
    out_shape = [jax.ShapeDtypeStruct((m, qw), BF16)]
    o_scr_rows = span if dil > 1 else 8
    if want_lse:
        out_specs.append(pl.BlockSpec((span, LANES), lambda g, r: (g, 0)))
        out_shape.append(jax.ShapeDtypeStruct((m, LANES), F32))
    return pl.pallas_call(
        functools.partial(_attn_kernel, n_heads=n_heads, gqa=gqa, max_dist=max_dist, dil=dil, n_res=n_res,
                          n_blocks=n_blocks, spans_per_batch=spans_per_batch,
                          has_sink=sink is not None, want_lse=want_lse),
        grid=(m // span, steps),
        in_specs=in_specs,
        out_specs=out_specs,
        out_shape=out_shape,
        scratch_shapes=[pltpu.VMEM((LANES, BLOCK), F32), pltpu.VMEM((qw // LANES, o_scr_rows, LANES), F32)],
        compiler_params=_params(2),
        name="banded_attention",
    )(*args)


def _retention_kernel(q_ref, k_ref, v_ref, gate_ref, o_ref, state_ref, dec_ref):
    n = pl.program_id(0)
    n_pairs = C_HEADS // 2
    log_g = [math.log1p(-2.0 ** (-5.0 - h)) for h in range(C_HEADS)]
    row = lax.broadcasted_iota(jnp.int32, (C_CHUNK, LANES), 0)
    lane = lax.broadcasted_iota(jnp.int32, (C_CHUNK, LANES), 1)
    low = lane < C_KEY_DIM

    @pl.when(n == 0)
    def _():
        state_ref[...] = jnp.zeros_like(state_ref)
        rel = (row - lane).astype(F32)
        idx = row.astype(F32)
        for h in range(C_HEADS):
            dec_ref[0, h] = jnp.where(rel >= 0, jnp.exp(log_g[h] * jnp.maximum(rel, 0.0)), 0.0)
            dec_ref[1, h] = jnp.exp(log_g[h] * (idx + 1.0))
        for p in range(n_pairs):
            lg = jnp.where(low, log_g[2 * p], log_g[2 * p + 1])
            dec_ref[2, p] = jnp.exp(lg * (C_CHUNK - 1.0 - idx))

    units = [(b, p) for b in range(q_ref.shape[0]) for p in range(n_pairs)]

    def scores(u):
        b, p = units[u]
        cols = slice(p * LANES, (p + 1) * LANES)
        qp, kp = q_ref[b, :, cols], k_ref[b, :, cols]
        zero = jnp.zeros_like(qp)
        lhs = jnp.concatenate([jnp.where(low, qp, zero), jnp.where(low, zero, qp)], axis=0)
        s = lax.dot_general(lhs, kp, NT, preferred_element_type=F32)
        qs = jnp.dot(lhs, state_ref[b, p].astype(BF16), preferred_element_type=F32)
        return s, qs, kp

    def decay(u, sc):
        _, p = units[u]
        s, qs, kp = sc
        a = [(s[hh * C_CHUNK:(hh + 1) * C_CHUNK] * dec_ref[0, 2 * p + hh]).astype(BF16) for hh in range(2)]
        kd_t = (kp.astype(F32) * dec_ref[2, p]).T.astype(BF16)
        return a, kd_t, qs

    def values(u, dc):
        b, p = units[u]
        a, kd_t, qs = dc
        outs = []
        for hh in range(2):
            h = 2 * p + hh
            v = v_ref[b, :, h * C_VAL_DIM:(h + 1) * C_VAL_DIM]
            srows = slice(hh * C_KEY_DIM, (hh + 1) * C_KEY_DIM)
            res = jnp.dot(jnp.concatenate([a[hh], kd_t[srows, :]], axis=0), v, preferred_element_type=F32)
            outs.append(res[:C_CHUNK] + qs[hh * C_CHUNK:(hh + 1) * C_CHUNK] * dec_ref[1, h])
            state_ref[b, p, srows, :] = (state_ref[b, p, srows, :] * math.exp(log_g[h] * C_CHUNK)
                                         + res[C_CHUNK:])
        return outs

    def store(u, outs):
        b, p = units[u]
        for hh in range(2):
            h = 2 * p + hh
            vs = slice(h * C_VAL_DIM, (h + 1) * C_VAL_DIM)
            o = outs[hh]
            mu = jnp.mean(o, axis=-1, keepdims=True)
            oc = o - mu
            var = jnp.mean(oc * oc, axis=-1, keepdims=True)
            gate = gate_ref[b, :, vs].astype(F32)
            o_ref[b, :, vs] = (gate * jax.nn.sigmoid(gate) * (oc * lax.rsqrt(var + LN_EPS))).astype(o_ref.dtype)

    _software_pipeline(len(units), scores, decay, values, lambda u, outs, _: store(u, outs), lags=RET_LAGS)


def retention_gated(q, k, v, gate):
    bsz, seq, _ = q.shape
    blk = lambda w: pl.BlockSpec((bsz, C_CHUNK, w), lambda n: (0, n, 0))
    return pl.pallas_call(
        _retention_kernel,
        grid=(seq // C_CHUNK,),
        in_specs=[blk(C_QK), blk(C_QK), blk(C_V), blk(C_V)],
        out_specs=blk(C_V),
        out_shape=jax.ShapeDtypeStruct((bsz, seq, C_V), BF16),
        scratch_shapes=[pltpu.VMEM((bsz, C_HEADS // 2, 2 * C_KEY_DIM, C_VAL_DIM), F32),
                        pltpu.VMEM((3, C_HEADS, C_CHUNK, LANES), F32)],
        compiler_params=_params(1),
        name="retention",
    )(q, k, v, gate)


def _merge_kernel(x_ref, xb_ref, o1_ref, o2_ref, o3_ref, l1_ref, l2_ref, l3_ref, yb_ref, yc_ref,
                  wg_ref, gb_ref, pa_ref, pb_ref, pc_ref, wo_ref, e_ref, g_ref, b_ref, out_ref):
    expand = e_ref[...]
    lane = lax.broadcasted_iota(jnp.int32, (MERGE_SUB, LANES), 1)

    def per_lane(w):
        w = jnp.where(lane < A_HEADS, w, 0.0)
        hi = w.astype(BF16).astype(F32)
        packed = hi + pltpu.roll(w - hi, LANES // 2, axis=1)
        return jnp.dot(packed.astype(BF16), expand, preferred_element_type=F32)

    for sub in range(xb_ref.shape[0] // MERGE_SUB):
        rows = slice(sub * MERGE_SUB, (sub + 1) * MERGE_SUB)
        x = jnp.concatenate([x_ref[j, rows, :] for j in range(N_SLABS)], axis=1)
        xb = xb_ref[rows, :]
        l1, l2, l3 = l1_ref[rows, :], l2_ref[rows, :], l3_ref[rows, :]
        m = jnp.maximum(jnp.maximum(l1, l2), l3)
        e1, e2, e3 = jnp.exp(l1 - m), jnp.exp(l2 - m), jnp.exp(l3 - m)
        inv = 1.0 / (e1 + e2 + e3)
        o3 = o3_ref[rows, :].astype(F32)
        ya = (o3 + per_lane(e1 * inv) * (o1_ref[rows, :].astype(F32) - o3)
              + per_lane(e2 * inv) * (o2_ref[rows, :].astype(F32) - o3))

        def gate(j, xb=xb):
            cols = slice(j * D_MODEL, (j + 1) * D_MODEL)
            return jax.nn.sigmoid(jnp.dot(xb, wg_ref[:, cols], preferred_element_type=F32) + gb_ref[:, cols])

        merged = gate(0) * jnp.dot(ya.astype(BF16), pa_ref[...], preferred_element_type=F32)
        merged += gate(1) * jnp.dot(yb_ref[rows, :], pb_ref[...], preferred_element_type=F32)
        merged += gate(2) * jnp.dot(yc_ref[rows, :], pc_ref[...], preferred_element_type=F32)
        mix = jnp.dot(merged.astype(BF16), wo_ref[...], preferred_element_type=F32)
        out_ref[rows, :] = _layer_norm(ALPHA * x + mix, g_ref[...], b_ref[...])


def merge_ln(x_slabs, xb, o1, o2, o3, l1, l2, l3, yb, yc, wg, gb, pa, pb, pc, wo, g, b, tm=512):
    m = xb.shape[0]
    head = jnp.arange(A_W) // HEAD_DIM
    expand = ((jnp.arange(LANES)[:, None] % (LANES // 2)) == head[None, :]).astype(BF16)
    row = lambda w: pl.BlockSpec((tm, w), lambda i: (i, 0))
    return pl.pallas_call(
        _merge_kernel,
        grid=(m // tm,),
        in_specs=[pl.BlockSpec((N_SLABS, tm, LANES), lambda i: (0, i, 0)), row(D_MODEL),
                  row(A_W), row(A_W), row(A_W), row(LANES), row(LANES), row(LANES), row(B_QW), row(C_V),
                  _full(wg.shape), _full(gb.shape), _full(pa.shape),
                  _full(pb.shape), _full(pc.shape), _full(wo.shape), _full(expand.shape),
                  _full((1, D_MODEL)), _full((1, D_MODEL))],
        out_specs=row(D_MODEL),
        out_shape=jax.ShapeDtypeStruct((m, D_MODEL), F32),
        compiler_params=_params(1),
        name="merge_ln",
    )(x_slabs, xb, o1, o2, o3, l1, l2, l3, yb, yc, wg, gb, pa, pb, pc, wo, expand, g, b)


def kernel(x, positions, w_in, gate_bias, attn_sinks, w_proj_a, w_proj_b, w_proj_c, w_out, ffn1_up, ffn1_down, ffn2_up, ffn2_down, ln1_g, ln1_b, ln2_g, ln2_b, ln3_g, ln3_b):
    bsz, seq, _ = x.shape
    m = bsz * seq
    dils = tuple(d for _, d in A_GROUPS)
    tables = {1: rope_tables(positions.reshape(m, 1))}
    tables.update(zip(dils[1:], permute_tables(*tables[1], dils[1:])))
    x = x.reshape(m, D_MODEL)
    a_end = 9 * A_W
    b_end = a_end + B_QW + 2 * B_KVW
    c_end = b_end + 2 * C_QK + 2 * C_V
    qk_scale = HEAD_DIM ** -0.5 * LOG2_E
    seg_a = ((A_W, True, qk_scale), (A_W, True, 1.0))
    seg_b = ((B_QW, True, qk_scale), (B_KVW, True, 1.0))
    seg_c = ((C_QK, True, 1.0), (C_QK, True, C_KEY_DIM ** -0.5), (C_V, False, 1.0), (C_V, False, 1.0))
    row = lambda t: t.reshape(1, -1)
    for l in range(DEPTH):
        x_slabs, xb = ffn_ln(x, ffn1_up[l].astype(BF16), ffn1_down[l].astype(BF16), row(ln1_g[l]), row(ln1_b[l]), True)
        xb_by_dil = dict(zip(dils[1:], permute_tokens(x_slabs, dils[1:])))
        xb_by_dil[1] = xb
        outs, lses = [], []
        for gi, (window, dil) in enumerate(A_GROUPS):
            q, k, vt = proj_rope(xb_by_dil[dil], w_in, l, 3 * gi * A_W, A_W, A_W, *tables[dil], seg_a, "proj_a")
            o, lse = banded_attention(q, k, vt, window // dil, dil, seq)
            outs.append(o)
            lses.append(lse)
        qb, kb, vtb = proj_rope(xb, w_in, l, a_end, MXU_COLS, B_KVW, *tables[1], seg_b, "proj_b")
        yb, = banded_attention(qb, kb, vtb, B_WINDOW - 1, 1, seq, attn_sinks[l], want_lse=False)
        qc, kc, vc, gc = proj_rope(xb, w_in, l, b_end, D_MODEL, 0, *tables[1], seg_c, "proj_c")
        yc = retention_gated(qc.reshape(bsz, seq, C_QK), kc.reshape(bsz, seq, C_QK),
                             vc.reshape(bsz, seq, C_V), gc.reshape(bsz, seq, C_V))
        x = merge_ln(x_slabs, xb, *outs, *lses, yb, yc.reshape(m, C_V),
                     w_in[l, :, c_end:].astype(BF16), row(gate_bias[l]), w_proj_a[l].astype(BF16),
                     w_proj_b[l].astype(BF16), w_proj_c[l].astype(BF16), w_out[l].astype(BF16),
                     row(ln2_g[l]), row(ln2_b[l]))
        x = ffn_ln(x, ffn2_up[l].astype(BF16), ffn2_down[l].astype(BF16), row(ln3_g[l]), row(ln3_b[l]), False)
    return x.reshape(bsz, seq, D_MODEL)
```

```python
import functools
import math

import jax
import jax.numpy as jnp
from jax import lax
from jax.experimental import pallas as pl
from jax.experimental.pallas import tpu as pltpu

D_MODEL = 1024
DEPTH = 2
HEAD_DIM = 64
BLOCK = 128
A_GROUPS = ((128, 1), (512, 4), (2048, 16))
A_HEADS = 12
A_W = A_HEADS * HEAD_DIM
B_Q_HEADS = 16
B_KV_HEADS = 2
B_WINDOW = 128
B_QW = B_Q_HEADS * HEAD_DIM
B_KVW = B_KV_HEADS * HEAD_DIM
C_HEADS = 8
C_KEY_DIM = 64
C_VAL_DIM = 128
C_CHUNK = 128
C_QK = C_HEADS * C_KEY_DIM
C_V = C_HEADS * C_VAL_DIM
D_FF = 2816
ROPE_THETA = 10000.0
LN_EPS = 1e-5
ALPHA = (2.0 * DEPTH) ** 0.25

LANES = 128
MXU_COLS = 256
FF_CHUNK = 256
FFN_SUB = 512
MERGE_SUB = 256
SPAN = 2048
PLAIN_SPAN = 1024
ATTN_UNITS_BLOCKS = 8
ATTN_LAGS = (3, 4, 7)
RET_LAGS = (1, 2, 3)
NEG_BIG = -1e30
ONES_ROWS = 16
LOG2_E = math.log2(math.e)
LN_2 = math.log(2.0)
VMEM_LIMIT = 56 * 1024 * 1024
N_SLABS = D_MODEL // LANES

F32 = jnp.float32
BF16 = jnp.bfloat16
NT = (((1,), (1,)), ((), ()))


def _params(n_axes):
    return pltpu.CompilerParams(dimension_semantics=("arbitrary",) * n_axes,
                                vmem_limit_bytes=VMEM_LIMIT)


def _layer_norm(y, g, b):
    mu = jnp.mean(y, axis=-1, keepdims=True)
    yc = y - mu
    var = jnp.mean(yc * yc, axis=-1, keepdims=True)
    return yc * lax.rsqrt(var + LN_EPS) * g + b


def _full(shape):
    nd = len(shape)
    return pl.BlockSpec(shape, lambda *_: (0,) * nd, pipeline_mode=pl.Buffered(1))


def _software_pipeline(n, stage_a, stage_b, stage_c, stage_d, lags=(1, 2, 3)):
    a_val, b_val, c_val = {}, {}, {}
    lag_b, lag_c, lag_d = lags
    for t in range(n + lag_d):
        if t < n:
            a_val[t] = stage_a(t)
        if 0 <= t - lag_b < n:
            b_val[t - lag_b] = stage_b(t - lag_b, a_val.pop(t - lag_b))
        if 0 <= t - lag_c < n:
            c_val[t - lag_c] = stage_c(t - lag_c, b_val[t - lag_c])
        if 0 <= t - lag_d < n:
            stage_d(t - lag_d, c_val.pop(t - lag_d), b_val.pop(t - lag_d))


def _rope_table_kernel(pos_ref, inv_ref, sign_ref, cos_ref, sin_ref):
    ang = pos_ref[...].astype(F32) * inv_ref[...]
    cos_ref[...] = jnp.cos(ang)
    sin_ref[...] = jnp.sin(ang) * sign_ref[...]


def rope_tables(pos_col, tm=1024):
    m = pos_col.shape[0]
    half = HEAD_DIM // 2
    inv = ROPE_THETA ** (-jnp.arange(half, dtype=F32) / half)
    inv_row = jnp.tile(inv, LANES // half)[None, :]
    lane = jnp.arange(LANES)
    sign_row = jnp.where(lane % HEAD_DIM < half, -1.0, 1.0).astype(F32)[None, :]
    return pl.pallas_call(
        _rope_table_kernel,
        grid=(m // tm,),
        in_specs=[pl.BlockSpec((tm, 1), lambda i: (i, 0)), _full((1, LANES)), _full((1, LANES))],
        out_specs=[pl.BlockSpec((tm, LANES), lambda i: (i, 0))] * 2,
        out_shape=[jax.ShapeDtypeStruct((m, LANES), F32)] * 2,
        compiler_params=_params(1),
        name="rope_table",
    )(pos_col, inv_row, sign_row)


def _ffn_kernel(x_ref, wu_ref, wd_ref, g_ref, b_ref, *out_refs, slab_out):
    n_chunks = D_FF // FF_CHUNK
    for sub in range(x_ref.shape[0] // FFN_SUB):
        rows = slice(sub * FFN_SUB, (sub + 1) * FFN_SUB)
        x = x_ref[rows, :]
        xb = x.astype(BF16)

        def hidden(c, xb=xb):
            lo = c * FF_CHUNK
            a = jnp.dot(xb, wu_ref[:, lo:lo + FF_CHUNK], preferred_element_type=F32)
            b = jnp.dot(xb, wu_ref[:, D_FF + lo:D_FF + lo + FF_CHUNK], preferred_element_type=F32)
            return (a * jax.nn.sigmoid(a) * b).astype(BF16)

        h = hidden(0)
        acc = None
        for c in range(n_chunks):
            h_next = hidden(c + 1) if c + 1 < n_chunks else None
            d = jnp.dot(h, wd_ref[c * FF_CHUNK:(c + 1) * FF_CHUNK, :], preferred_element_type=F32)
            acc = d if acc is None else acc + d
            h = h_next
        y = _layer_norm(ALPHA * x + 0.5 * acc, g_ref[...], b_ref[...])
        if slab_out:
            slab_ref, bf_ref = out_refs
            for j in range(N_SLABS):
                slab_ref[j, rows, :] = y[:, j * LANES:(j + 1) * LANES]
            bf_ref[rows, :] = y.astype(BF16)
        else:
            out_refs[0][rows, :] = y


def ffn_ln(x, wu, wd, g, b, slab_out, tm=1024):
    m = x.shape[0]
    row = pl.BlockSpec((tm, D_MODEL), lambda i: (i, 0))
    if slab_out:
        out_specs = [pl.BlockSpec((N_SLABS, tm, LANES), lambda i: (0, i, 0)), row]
        out_shape = [jax.ShapeDtypeStruct((N_SLABS, m, LANES), F32), jax.ShapeDtypeStruct((m, D_MODEL), BF16)]
    else:
        out_specs = row
        out_shape = jax.ShapeDtypeStruct((m, D_MODEL), F32)
    return pl.pallas_call(
        functools.partial(_ffn_kernel, slab_out=slab_out),
        grid=(m // tm,),
        in_specs=[row, _full(wu.shape), _full(wd.shape), _full((1, D_MODEL)), _full((1, D_MODEL))],
        out_specs=out_specs,
        out_shape=out_shape,
        compiler_params=_params(1),
        name="ffn_ln",
    )(x, wu, wd, g, b)


def _permute_kernel(x_ref, *out_refs, dils):
    for o_ref, d in zip(out_refs, dils):
        chunk = SPAN // d
        for r in range(d):
            for j in range(N_SLABS):
                o_ref[r * chunk:(r + 1) * chunk, j * LANES:(j + 1) * LANES] = (
                    x_ref[j, pl.ds(r, chunk, stride=d), :].astype(BF16))


def permute_tokens(x_slabs, dils):
    m = x_slabs.shape[1]
    return pl.pallas_call(
        functools.partial(_permute_kernel, dils=dils),
        grid=(m // SPAN,),
        in_specs=[pl.BlockSpec((N_SLABS, SPAN, LANES), lambda i: (0, i, 0))],
        out_specs=[pl.BlockSpec((SPAN, D_MODEL), lambda i: (i, 0)) for _ in dils],
        out_shape=[jax.ShapeDtypeStruct((m, D_MODEL), BF16) for _ in dils],
        compiler_params=_params(1),
        name="permute_tokens",
    )(x_slabs)


def _permute_table_kernel(cos_ref, sin_ref, *out_refs, dils):
    outs = iter(out_refs)
    for d in dils:
        chunk = SPAN // d
        for t_ref in (cos_ref, sin_ref):
            o_ref = next(outs)
            for r in range(d):
                o_ref[r * chunk:(r + 1) * chunk, :] = t_ref[pl.ds(r, chunk, stride=d), :]


def permute_tables(cos, sin_signed, dils):
    m = cos.shape[0]
    blk = pl.BlockSpec((SPAN, LANES), lambda i: (i, 0))
    outs = pl.pallas_call(
        functools.partial(_permute_table_kernel, dils=dils),
        grid=(m // SPAN,),
        in_specs=[blk, blk],
        out_specs=[blk] * (2 * len(dils)),
        out_shape=[jax.ShapeDtypeStruct((m, LANES), F32)] * (2 * len(dils)),
        compiler_params=_params(1),
        name="permute_tables",
    )(cos, sin_signed)
    return [(outs[2 * i], outs[2 * i + 1]) for i in range(len(dils))]


def _rope_slab(y, cos, sin_signed, first_half):
    partner = jnp.where(first_half, pltpu.roll(y, LANES - HEAD_DIM // 2, axis=1),
                        pltpu.roll(y, HEAD_DIM // 2, axis=1))
    return y * cos + partner * sin_signed


def _proj_kernel(*refs, n_w, plan, segs, t_width):
    x_ref, w_refs = refs[0], refs[1:1 + n_w]
    cos_ref, sin_ref = refs[1 + n_w:3 + n_w]
    out_refs = refs[3 + n_w:-2]
    w_scr, wt_scr = refs[-2:]

    @pl.when(pl.program_id(0) == 0)
    def _():
        for idx, lo, hi, transposed, dst in plan:
            blk = w_refs[idx][:, lo:hi]
            if transposed:
                wt_scr[dst:dst + hi - lo, :] = blk.T.astype(BF16)
            else:
                w_scr[:, dst:dst + hi - lo] = blk.astype(BF16)

    xb = x_ref[...]
    cos = cos_ref[...]
    sin_signed = sin_ref[...]
    lane = lax.broadcasted_iota(jnp.int32, cos.shape, 1)
    first_half = (lane % HEAD_DIM) < (HEAD_DIM // 2)
    col = 0
    for o_ref, (width, rope, scale) in zip(out_refs, segs):
        for c0 in range(0, width, MXU_COLS):
            cw = min(MXU_COLS, width - c0)
            y2 = jnp.dot(xb, w_scr[:, col + c0:col + c0 + cw], preferred_element_type=F32)
            for s0 in range(0, cw, LANES):
                y = y2[:, s0:s0 + LANES]
                if rope:
                    y = _rope_slab(y, cos, sin_signed, first_half)
                if scale != 1.0:
                    y = y * scale
                o_ref[:, c0 + s0:c0 + s0 + LANES] = y.astype(o_ref.dtype)
        col += width
    if t_width:
        t_ref = out_refs[len(segs)]
        for c0 in range(0, t_width, MXU_COLS):
            cw = min(MXU_COLS, t_width - c0)
            t_ref[c0:c0 + cw, :] = lax.dot_general(wt_scr[c0:c0 + cw, :], xb, NT,
                                                   preferred_element_type=F32).astype(t_ref.dtype)


def proj_rope(xb, w_in, layer, col0, block_w, t_width, cos, sin_signed, segs, name, tm=1024):
    m = xb.shape[0]
    n_row = sum(s[0] for s in segs)
    total = n_row + t_width
    assert col0 % block_w == 0 and total % block_w == 0
    n_w = total // block_w
    plan = []
    for idx in range(n_w):
        lo, hi = idx * block_w, (idx + 1) * block_w
        if lo < n_row:
            plan.append((idx, 0, min(hi, n_row) - lo, False, lo))
        if hi > n_row:
            start = max(lo, n_row)
            plan.append((idx, start - lo, block_w, True, start - n_row))
    w_specs = [pl.BlockSpec((None, D_MODEL, block_w), lambda i, cb=col0 // block_w + idx: (layer, 0, cb),
                            pipeline_mode=pl.Buffered(1)) for idx in range(n_w)]
    row = lambda wd: pl.BlockSpec((tm, wd), lambda i: (i, 0))
    out_specs = [row(s[0]) for s in segs]
    out_shape = [jax.ShapeDtypeStruct((m, s[0]), BF16) for s in segs]
    if t_width:
        out_specs.append(pl.BlockSpec((t_width, tm), lambda i: (0, i)))
        out_shape.append(jax.ShapeDtypeStruct((t_width, m), BF16))
    return pl.pallas_call(
        functools.partial(_proj_kernel, n_w=n_w, plan=tuple(plan), segs=segs, t_width=t_width),
        grid=(m // tm,),
        in_specs=[row(D_MODEL), *w_specs, row(LANES), row(LANES)],
        out_specs=out_specs,
        out_shape=out_shape,
        scratch_shapes=[pltpu.VMEM((D_MODEL, n_row), BF16), pltpu.VMEM((max(t_width, 16), D_MODEL), BF16)],
        compiler_params=_params(1),
        name=name,
    )(xb, *([w_in] * n_w), cos, sin_signed)


def _attn_kernel(*refs, n_heads, gqa, max_dist, dil, n_res, n_blocks, spans_per_batch, has_sink, want_lse):
    refs = list(refs)
    sink_ref = refs.pop(0) if has_sink else None
    n_halo = 1 if n_blocks == 1 else n_res
    q_ref, kc_ref, vc_ref = refs[:3]
    kh_refs, vh_refs = refs[3:3 + n_halo], refs[3 + n_halo:3 + 2 * n_halo]
    o_ref = refs[3 + 2 * n_halo]
    lse_ref = refs[4 + 2 * n_halo] if want_lse else None
    lse_scr, o_scr = refs[-2:]
    g = pl.program_id(0)
    r = pl.program_id(1)
    first_has_prev = (g % spans_per_batch) > 0
    key = lax.broadcasted_iota(jnp.int32, (2 * BLOCK, 2 * BLOCK), 0)
    qry = lax.broadcasted_iota(jnp.int32, (2 * BLOCK, 2 * BLOCK), 1) % BLOCK
    in_band_prev = jnp.logical_and(key < BLOCK, key >= qry + (BLOCK - max_dist))
    in_band_cur = jnp.logical_and(key >= BLOCK, key - BLOCK <= qry)
    bias_on = jnp.where(jnp.logical_or(in_band_prev, in_band_cur), 0.0, NEG_BIG)
    bias_first = jnp.where(jnp.logical_or(jnp.logical_and(in_band_prev, first_has_prev), in_band_cur),
                           0.0, NEG_BIG)
    low = lax.broadcasted_iota(jnp.int32, (BLOCK, LANES), 1) < HEAD_DIM
    lane2 = lax.broadcasted_iota(jnp.int32, (1, 2 * BLOCK), 1)
    ones_rows = jnp.ones((ONES_ROWS, 2 * BLOCK), BF16)
    n_pairs = n_heads // 2
    rep = n_heads // (kc_ref.shape[1] // HEAD_DIM)
    units = [(j, i, p) for j in range(n_res) for i in range(n_blocks) for p in range(n_pairs)]

    def rows_of(j, i):
        return slice((j * n_blocks + i) * BLOCK, (j * n_blocks + i + 1) * BLOCK)

    def prev_and_cur(j, i):
        return slice((j * n_blocks + i - 1) * BLOCK, (j * n_blocks + i + 1) * BLOCK)

    def halo_k(j, kcols):
        return kh_refs[0][j * BLOCK:(j + 1) * BLOCK, kcols] if n_halo == 1 else kh_refs[j][:, kcols]

    def halo_vt(j, vrows):
        return vh_refs[0][vrows, j * BLOCK:(j + 1) * BLOCK] if n_halo == 1 else vh_refs[j][vrows, :]

    def scores(u):
        j, i, p = units[u]
        rows = rows_of(j, i)
        qp = q_ref[rows, p * LANES:(p + 1) * LANES]
        zero = jnp.zeros_like(qp)
        if gqa:
            kv = (2 * p) // rep
            qr = pltpu.roll(qp, HEAD_DIM, axis=1)
            first, second = (qp, qr) if kv == 0 else (qr, qp)
            keep = low if kv == 0 else jnp.logical_not(low)
            rhs = jnp.concatenate([jnp.where(keep, first, zero), jnp.where(keep, second, zero)], axis=0)
            kcols = slice(0, LANES)
        else:
            rhs = jnp.concatenate([jnp.where(low, qp, zero), jnp.where(low, zero, qp)], axis=0)
            kcols = slice(p * LANES, (p + 1) * LANES)
        if i == 0:
            keys = jnp.concatenate([halo_k(j, kcols), kc_ref[rows, kcols]], axis=0)
            bias = bias_first
        else:
            keys, bias = kc_ref[prev_and_cur(j, i), kcols], bias_on
        return lax.dot_general(keys, rhs, NT, preferred_element_type=F32) + bias

    def softmax(u, s):
        p = units[u][2]
        m = jnp.max(s, axis=0, keepdims=True)
        sink_row = None
        if has_sink:
            sink_row = jnp.where(lane2 < BLOCK, sink_ref[2 * p], sink_ref[2 * p + 1]) * LOG2_E
            m = jnp.maximum(m, sink_row)
        return jnp.exp2(s - m).astype(BF16), m, sink_row

    def values(u, sm):
        j, i, p = units[u]
        vrows = (slice(((2 * p) // rep) * HEAD_DIM, ((2 * p) // rep + 1) * HEAD_DIM) if gqa
                 else slice(p * LANES, (p + 1) * LANES))
        if i == 0:
            vt = jnp.concatenate([halo_vt(j, vrows), vc_ref[vrows, rows_of(j, i)]], axis=1)
        else:
            vt = vc_ref[vrows, prev_and_cur(j, i)]
        return jnp.dot(jnp.concatenate([vt, ones_rows], axis=0), sm[0], preferred_element_type=F32)

    def store(u, res, sm):
        j, i, p = units[u]
        _, m, sink_row = sm
        n_feat = res.shape[0] - ONES_ROWS
        den = res[n_feat:n_feat + 1, :]
        if has_sink:
            den = den + jnp.exp2(sink_row - m)
        inv = 1.0 / den
        second = slice(0, HEAD_DIM) if gqa else slice(HEAD_DIM, 2 * HEAD_DIM)
        o_pair = jnp.concatenate([res[:HEAD_DIM, :BLOCK] * inv[:, :BLOCK],
                                  res[second, BLOCK:] * inv[:, BLOCK:]], axis=0)
        if dil == 1:
            dst = rows_of(j, i)
            o_ref[dst, p * LANES:(p + 1) * LANES] = o_pair.T.astype(o_ref.dtype)
        else:
            dst = pl.ds(r * n_res + j + dil * BLOCK * i, BLOCK, stride=dil)
            o_scr[p, dst, :] = o_pair.T
        if want_lse:
            lse = (m + jnp.log2(den)) * LN_2
            if p == 0:
                lse_scr[...] = jnp.zeros_like(lse_scr)
            lse_scr[2 * p:2 * p + 1, :] = lse[:, :BLOCK]
            lse_scr[2 * p + 1:2 * p + 2, :] = lse[:, BLOCK:]
            if p == n_pairs - 1:
                lse_ref[dst, :] = lse_scr[...].T

    _software_pipeline(len(units), scores, softmax, values, store, lags=ATTN_LAGS)

    if dil > 1:
        @pl.when(r == pl.num_programs(1) - 1)
        def _():
            for p in range(n_pairs):
                o_ref[:, p * LANES:(p + 1) * LANES] = o_scr[p].astype(o_ref.dtype)


def banded_attention(q, k, vt, max_dist, dil, seq, sink=None, want_lse=True):
    m, qw = q.shape
    kw = k.shape[1]
    n_heads = qw // HEAD_DIM
    gqa = kw != qw
    span = SPAN if dil > 1 else PLAIN_SPAN
    chunk = span // dil
    n_blocks = chunk // BLOCK
    n_res = max(ATTN_UNITS_BLOCKS // n_blocks, 1)
    spans_per_batch = seq // span
    steps = dil // n_res
    rows = chunk * n_res

    def cur(g, r):
        return (g * steps + r, 0)

    def cur_t(g, r):
        return (0, g * steps + r)

    def halo_block(g, r, j):
        return jnp.maximum(((g - 1) * dil + r * n_res + j + 1) * n_blocks - 1, 0)

    if n_blocks == 1:
        halo = BLOCK * n_res
        k_halos = [pl.BlockSpec((halo, kw), lambda g, r: (halo_block(g, r, 0) // n_res, 0))]
        v_halos = [pl.BlockSpec((kw, halo), lambda g, r: (0, halo_block(g, r, 0) // n_res))]
    else:
        k_halos = [pl.BlockSpec((BLOCK, kw), lambda g, r, j=j: (halo_block(g, r, j), 0)) for j in range(n_res)]
        v_halos = [pl.BlockSpec((kw, BLOCK), lambda g, r, j=j: (0, halo_block(g, r, j))) for j in range(n_res)]
    in_specs = [pl.BlockSpec((rows, qw), cur), pl.BlockSpec((rows, kw), cur), pl.BlockSpec((kw, rows), cur_t),
                *k_halos, *v_halos]
    args = [q, k, vt] + [k] * len(k_halos) + [vt] * len(v_halos)
    if sink is not None:
        in_specs = [pl.BlockSpec(memory_space=pltpu.SMEM)] + in_specs
        args = [sink] + args
    out_specs = [pl.BlockSpec((span, qw), lambda g, r: (g, 0))]
    out_shape = [jax.ShapeDtypeStruct((m, qw), BF16)]
    o_scr_rows = span if dil > 1 else 8
    if want_lse:
        out_specs.append(pl.BlockSpec((span, LANES), lambda g, r: (g, 0)))
        out_shape.append(jax.ShapeDtypeStruct((m, LANES), F32))
    return pl.pallas_call(
        functools.partial(_attn_kernel, n_heads=n_heads, gqa=gqa, max_dist=max_dist, dil=dil, n_res=n_res,
                          n_blocks=n_blocks, spans_per_batch=spans_per_batch,
                          has_sink=sink is not None, want_lse=want_lse),
        grid=(m // span, steps),
        in_specs=in_specs,
        out_specs=out_specs,
        out_shape=out_shape,
        scratch_shapes=[pltpu.VMEM((LANES, BLOCK), F32), pltpu.VMEM((qw // LANES, o_scr_rows, LANES), F32)],
        compiler_params=_params(2),
        name="banded_attention",
    )(*args)


def _retention_kernel(q_ref, k_ref, v_ref, gate_ref, o_ref, state_ref, dec_ref):
    n = pl.program_id(0)
    n_pairs = C_HEADS // 2
    log_g = [math.log1p(-2.0 ** (-5.0 - h)) for h in range(C_HEADS)]
    row = lax.broadcasted_iota(jnp.int32, (C_CHUNK, LANES), 0)
    lane = lax.broadcasted_iota(jnp.int32, (C_CHUNK, LANES), 1)
    low = lane < C_KEY_DIM

    @pl.when(n == 0)
    def _():
        state_ref[...] = jnp.zeros_like(state_ref)
        rel = (row - lane).astype(F32)
        idx = row.astype(F32)
        for h in range(C_HEADS):
            dec_ref[0, h] = jnp.where(rel >= 0, jnp.exp(log_g[h] * jnp.maximum(rel, 0.0)), 0.0)
            dec_ref[1, h] = jnp.exp(log_g[h] * (idx + 1.0))
        for p in range(n_pairs):
            lg = jnp.where(low, log_g[2 * p], log_g[2 * p + 1])
            dec_ref[2, p] = jnp.exp(lg * (C_CHUNK - 1.0 - idx))

    units = [(b, p) for b in range(q_ref.shape[0]) for p in range(n_pairs)]

    def scores(u):
        b, p = units[u]
        cols = slice(p * LANES, (p + 1) * LANES)
        qp, kp = q_ref[b, :, cols], k_ref[b, :, cols]
        zero = jnp.zeros_like(qp)
        lhs = jnp.concatenate([jnp.where(low, qp, zero), jnp.where(low, zero, qp)], axis=0)
        s = lax.dot_general(lhs, kp, NT, preferred_element_type=F32)
        qs = jnp.dot(lhs, state_ref[b, p].astype(BF16), preferred_element_type=F32)
        return s, qs, kp

    def decay(u, sc):
        _, p = units[u]
        s, qs, kp = sc
        a = [(s[hh * C_CHUNK:(hh + 1) * C_CHUNK] * dec_ref[0, 2 * p + hh]).astype(BF16) for hh in range(2)]
        kd_t = (kp.astype(F32) * dec_ref[2, p]).T.astype(BF16)
        return a, kd_t, qs

    def values(u, dc):
        b, p = units[u]
        a, kd_t, qs = dc
        outs = []
        for hh in range(2):
            h = 2 * p + hh
            v = v_ref[b, :, h * C_VAL_DIM:(h + 1) * C_VAL_DIM]
            srows = slice(hh * C_KEY_DIM, (hh + 1) * C_KEY_DIM)
            res = jnp.dot(jnp.concatenate([a[hh], kd_t[srows, :]], axis=0), v, preferred_element_type=F32)
            outs.append(res[:C_CHUNK] + qs[hh * C_CHUNK:(hh + 1) * C_CHUNK] * dec_ref[1, h])
            state_ref[b, p, srows, :] = (state_ref[b, p, srows, :] * math.exp(log_g[h] * C_CHUNK)
                                         + res[C_CHUNK:])
        return outs

    def store(u, outs):
        b, p = units[u]
        for hh in range(2):
            h = 2 * p + hh
            vs = slice(h * C_VAL_DIM, (h + 1) * C_VAL_DIM)
            o = outs[hh]
            mu = jnp.mean(o, axis=-1, keepdims=True)
            oc = o - mu
            var = jnp.mean(oc * oc, axis=-1, keepdims=True)
            gate = gate_ref[b, :, vs].astype(F32)
            o_ref[b, :, vs] = (gate * jax.nn.sigmoid(gate) * (oc * lax.rsqrt(var + LN_EPS))).astype(o_ref.dtype)

    _software_pipeline(len(units), scores, decay, values, lambda u, outs, _: store(u, outs), lags=RET_LAGS)


def retention_gated(q, k, v, gate):
    bsz, seq, _ = q.shape
    blk = lambda w: pl.BlockSpec((bsz, C_CHUNK, w), lambda n: (0, n, 0))
    return pl.pallas_call(
        _retention_kernel,
        grid=(seq // C_CHUNK,),
        in_specs=[blk(C_QK), blk(C_QK), blk(C_V), blk(C_V)],
        out_specs=blk(C_V),
        out_shape=jax.ShapeDtypeStruct((bsz, seq, C_V), BF16),
        scratch_shapes=[pltpu.VMEM((bsz, C_HEADS // 2, 2 * C_KEY_DIM, C_VAL_DIM), F32),
                        pltpu.VMEM((3, C_HEADS, C_CHUNK, LANES), F32)],
        compiler_params=_params(1),
        name="retention",
    )(q, k, v, gate)


def _merge_kernel(x_ref, xb_ref, o1_ref, o2_ref, o3_ref, l1_ref, l2_ref, l3_ref, yb_ref, yc_ref,
                  wg0_ref, wg1_ref, wg2_ref, gb_ref, pa_ref, pb_ref, pc_ref, wo_ref, e_ref, g_ref, b_ref,
                  out_ref, wg_ref):
    @pl.when(pl.program_id(0) == 0)
    def _():
        for j, blk_ref in enumerate((wg0_ref, wg1_ref, wg2_ref)):
            wg_ref[:, j * D_MODEL:(j + 1) * D_MODEL] = blk_ref[...].astype(BF16)

    expand = e_ref[...]
    lane = lax.broadcasted_iota(jnp.int32, (MERGE_SUB, LANES), 1)

    def per_lane(w):
        w = jnp.where(lane < A_HEADS, w, 0.0)
        hi = w.astype(BF16).astype(F32)
        packed = hi + pltpu.roll(w - hi, LANES // 2, axis=1)
        return jnp.dot(packed.astype(BF16), expand, preferred_element_type=F32)

    for sub in range(xb_ref.shape[0] // MERGE_SUB):
        rows = slice(sub * MERGE_SUB, (sub + 1) * MERGE_SUB)
        x = jnp.concatenate([x_ref[j, rows, :] for j in range(N_SLABS)], axis=1)
        xb = xb_ref[rows, :]
        l1, l2, l3 = l1_ref[rows, :], l2_ref[rows, :], l3_ref[rows, :]
        m = jnp.maximum(jnp.maximum(l1, l2), l3)
        e1, e2, e3 = jnp.exp(l1 - m), jnp.exp(l2 - m), jnp.exp(l3 - m)
        inv = 1.0 / (e1 + e2 + e3)
        o3 = o3_ref[rows, :].astype(F32)
        ya = (o3 + per_lane(e1 * inv) * (o1_ref[rows, :].astype(F32) - o3)
              + per_lane(e2 * inv) * (o2_ref[rows, :].astype(F32) - o3))

        def gate(j, xb=xb):
            cols = slice(j * D_MODEL, (j + 1) * D_MODEL)
            return jax.nn.sigmoid(jnp.dot(xb, wg_ref[:, cols], preferred_element_type=F32) + gb_ref[:, cols])

        merged = gate(0) * jnp.dot(ya.astype(BF16), pa_ref[...], preferred_element_type=F32)
        merged += gate(1) * jnp.dot(yb_ref[rows, :], pb_ref[...], preferred_element_type=F32)
        merged += gate(2) * jnp.dot(yc_ref[rows, :], pc_ref[...], preferred_element_type=F32)
        mix = jnp.dot(merged.astype(BF16), wo_ref[...], preferred_element_type=F32)
        out_ref[rows, :] = _layer_norm(ALPHA * x + mix, g_ref[...], b_ref[...])


def merge_ln(x_slabs, xb, o1, o2, o3, l1, l2, l3, yb, yc, w_in, layer, gate_col0, gb, pa, pb, pc, wo, g, b,
             tm=512):
    m = xb.shape[0]
    assert gate_col0 % D_MODEL == 0
    gate_specs = [pl.BlockSpec((None, D_MODEL, D_MODEL), lambda i, cb=gate_col0 // D_MODEL + j: (layer, 0, cb),
                               pipeline_mode=pl.Buffered(1)) for j in range(3)]
    head = jnp.arange(A_W) // HEAD_DIM
    expand = ((jnp.arange(LANES)[:, None] % (LANES // 2)) == head[None, :]).astype(BF16)
    row = lambda w: pl.BlockSpec((tm, w), lambda i: (i, 0))
    return pl.pallas_call(
        _merge_kernel,
        grid=(m // tm,),
        in_specs=[pl.BlockSpec((N_SLABS, tm, LANES), lambda i: (0, i, 0)), row(D_MODEL),
                  row(A_W), row(A_W), row(A_W), row(LANES), row(LANES), row(LANES), row(B_QW), row(C_V),
                  *gate_specs, _full(gb.shape), _full(pa.shape),
                  _full(pb.shape), _full(pc.shape), _full(wo.shape), _full(expand.shape),
                  _full((1, D_MODEL)), _full((1, D_MODEL))],
        out_specs=row(D_MODEL),
        out_shape=jax.ShapeDtypeStruct((m, D_MODEL), F32),
        scratch_shapes=[pltpu.VMEM((D_MODEL, 3 * D_MODEL), BF16)],
        compiler_params=_params(1),
        name="merge_ln",
    )(x_slabs, xb, o1, o2, o3, l1, l2, l3, yb, yc, w_in, w_in, w_in, gb, pa, pb, pc, wo, expand, g, b)


def kernel(x, positions, w_in, gate_bias, attn_sinks, w_proj_a, w_proj_b, w_proj_c, w_out, ffn1_up, ffn1_down, ffn2_up, ffn2_down, ln1_g, ln1_b, ln2_g, ln2_b, ln3_g, ln3_b):
    bsz, seq, _ = x.shape
    m = bsz * seq
    dils = tuple(d for _, d in A_GROUPS)
    tables = {1: rope_tables(positions.reshape(m, 1))}
    tables.update(zip(dils[1:], permute_tables(*tables[1], dils[1:])))
    x = x.reshape(m, D_MODEL)
    a_end = 9 * A_W
    b_end = a_end + B_QW + 2 * B_KVW
    c_end = b_end + 2 * C_QK + 2 * C_V
    qk_scale = HEAD_DIM ** -0.5 * LOG2_E
    seg_a = ((A_W, True, qk_scale), (A_W, True, 1.0))
    seg_b = ((B_QW, True, qk_scale), (B_KVW, True, 1.0))
    seg_c = ((C_QK, True, 1.0), (C_QK, True, C_KEY_DIM ** -0.5), (C_V, False, 1.0), (C_V, False, 1.0))
    row = lambda t: t.reshape(1, -1)
    for l in range(DEPTH):
        x_slabs, xb = ffn_ln(x, ffn1_up[l].astype(BF16), ffn1_down[l].astype(BF16), row(ln1_g[l]), row(ln1_b[l]), True)
        xb_by_dil = dict(zip(dils[1:], permute_tokens(x_slabs, dils[1:])))
        xb_by_dil[1] = xb
        outs, lses = [], []
        for gi, (window, dil) in enumerate(A_GROUPS):
            q, k, vt = proj_rope(xb_by_dil[dil], w_in, l, 3 * gi * A_W, A_W, A_W, *tables[dil], seg_a, "proj_a")
            o, lse = banded_attention(q, k, vt, window // dil, dil, seq)
            outs.append(o)
            lses.append(lse)
        qb, kb, vtb = proj_rope(xb, w_in, l, a_end, MXU_COLS, B_KVW, *tables[1], seg_b, "proj_b")
        yb, = banded_attention(qb, kb, vtb, B_WINDOW - 1, 1, seq, attn_sinks[l], want_lse=False)
        qc, kc, vc, gc = proj_rope(xb, w_in, l, b_end, D_MODEL, 0, *tables[1], seg_c, "proj_c")
        yc = retention_gated(qc.reshape(bsz, seq, C_QK), kc.reshape(bsz, seq, C_QK),
                             vc.reshape(bsz, seq, C_V), gc.reshape(bsz, seq, C_V))
        x = merge_ln(x_slabs, xb, *outs, *lses, yb, yc.reshape(m, C_V),
                     w_in, l, c_end, row(gate_bias[l]), w_proj_a[l].astype(BF16),
                     w_proj_b[l].astype(BF16), w_proj_c[l].astype(BF16), w_out[l].astype(BF16),
                     row(ln2_g[l]), row(ln2_b[l]))
        x = ffn_ln(x, ffn2_up[l].astype(BF16), ffn2_down[l].astype(BF16), row(ln3_g[l]), row(ln3_b[l]), False)
    return x.reshape(bsz, seq, D_MODEL)
```

```python
import functools
import math

import jax
import jax.numpy as jnp
from jax import lax
from jax.experimental import pallas as pl
from jax.experimental.pallas import tpu as pltpu

D_MODEL = 1024
DEPTH = 2
HEAD_DIM = 64
BLOCK = 128
A_GROUPS = ((128, 1), (512, 4), (2048, 16))
A_HEADS = 12
A_W = A_HEADS * HEAD_DIM
B_Q_HEADS = 16
B_KV_HEADS = 2
B_WINDOW = 128
B_QW = B_Q_HEADS * HEAD_DIM
B_KVW = B_KV_HEADS * HEAD_DIM
C_HEADS = 8
C_KEY_DIM = 64
C_VAL_DIM = 128
C_CHUNK = 128
C_QK = C_HEADS * C_KEY_DIM
C_V = C_HEADS * C_VAL_DIM
D_FF = 2816
ROPE_THETA = 10000.0
LN_EPS = 1e-5
ALPHA = (2.0 * DEPTH) ** 0.25

LANES = 128
MXU_COLS = 256
FF_CHUNK = 256
FFN_SUB = 512
MERGE_SUB = 256
SPAN = 2048
PLAIN_SPAN = 1024
ATTN_UNITS_BLOCKS = 8
ATTN_LAGS = (3, 4, 7)
RET_LAGS = (1, 2, 3)
NEG_BIG = -1e30
ONES_ROWS = 16
LOG2_E = math.log2(math.e)
LN_2 = math.log(2.0)
VMEM_LIMIT = 56 * 1024 * 1024
N_SLABS = D_MODEL // LANES

F32 = jnp.float32
BF16 = jnp.bfloat16
NT = (((1,), (1,)), ((), ()))


def _params(n_axes):
    return pltpu.CompilerParams(dimension_semantics=("arbitrary",) * n_axes,
                                vmem_limit_bytes=VMEM_LIMIT)


def _layer_norm(y, g, b):
    mu = jnp.mean(y, axis=-1, keepdims=True)
    yc = y - mu
    var = jnp.mean(yc * yc, axis=-1, keepdims=True)
    return yc * lax.rsqrt(var + LN_EPS) * g + b


def _full(shape):
    nd = len(shape)
    return pl.BlockSpec(shape, lambda *_: (0,) * nd, pipeline_mode=pl.Buffered(1))


def _layer_of(stacked, layer):
    nd = stacked.ndim - 1
    return pl.BlockSpec((None,) + stacked.shape[1:], lambda *_: (layer,) + (0,) * nd,
                        pipeline_mode=pl.Buffered(1))


def _software_pipeline(n, stage_a, stage_b, stage_c, stage_d, lags=(1, 2, 3)):
    a_val, b_val, c_val = {}, {}, {}
    lag_b, lag_c, lag_d = lags
    for t in range(n + lag_d):
        if t < n:
            a_val[t] = stage_a(t)
        if 0 <= t - lag_b < n:
            b_val[t - lag_b] = stage_b(t - lag_b, a_val.pop(t - lag_b))
        if 0 <= t - lag_c < n:
            c_val[t - lag_c] = stage_c(t - lag_c, b_val[t - lag_c])
        if 0 <= t - lag_d < n:
            stage_d(t - lag_d, c_val.pop(t - lag_d), b_val.pop(t - lag_d))


def _rope_table_kernel(pos_ref, inv_ref, sign_ref, cos_ref, sin_ref):
    ang = pos_ref[...].astype(F32) * inv_ref[...]
    cos_ref[...] = jnp.cos(ang)
    sin_ref[...] = jnp.sin(ang) * sign_ref[...]


def rope_tables(pos_col, tm=1024):
    m = pos_col.shape[0]
    half = HEAD_DIM // 2
    inv = ROPE_THETA ** (-jnp.arange(half, dtype=F32) / half)
    inv_row = jnp.tile(inv, LANES // half)[None, :]
    lane = jnp.arange(LANES)
    sign_row = jnp.where(lane % HEAD_DIM < half, -1.0, 1.0).astype(F32)[None, :]
    return pl.pallas_call(
        _rope_table_kernel,
        grid=(m // tm,),
        in_specs=[pl.BlockSpec((tm, 1), lambda i: (i, 0)), _full((1, LANES)), _full((1, LANES))],
        out_specs=[pl.BlockSpec((tm, LANES), lambda i: (i, 0))] * 2,
        out_shape=[jax.ShapeDtypeStruct((m, LANES), F32)] * 2,
        compiler_params=_params(1),
        name="rope_table",
    )(pos_col, inv_row, sign_row)


def _ffn_kernel(x_ref, wu_ref, wd_ref, g_ref, b_ref, *out_refs, slab_out):
    n_chunks = D_FF // FF_CHUNK
    for sub in range(x_ref.shape[0] // FFN_SUB):
        rows = slice(sub * FFN_SUB, (sub + 1) * FFN_SUB)
        x = x_ref[rows, :]
        xb = x.astype(BF16)

        def hidden(c, xb=xb):
            lo = c * FF_CHUNK
            a = jnp.dot(xb, wu_ref[:, lo:lo + FF_CHUNK], preferred_element_type=F32)
            b = jnp.dot(xb, wu_ref[:, D_FF + lo:D_FF + lo + FF_CHUNK], preferred_element_type=F32)
            return (a * jax.nn.sigmoid(a) * b).astype(BF16)

        h = hidden(0)
        acc = None
        for c in range(n_chunks):
            h_next = hidden(c + 1) if c + 1 < n_chunks else None
            d = jnp.dot(h, wd_ref[c * FF_CHUNK:(c + 1) * FF_CHUNK, :], preferred_element_type=F32)
            acc = d if acc is None else acc + d
            h = h_next
        y = _layer_norm(ALPHA * x + 0.5 * acc, g_ref[...], b_ref[...])
        if slab_out:
            slab_ref, bf_ref = out_refs
            for j in range(N_SLABS):
                slab_ref[j, rows, :] = y[:, j * LANES:(j + 1) * LANES]
            bf_ref[rows, :] = y.astype(BF16)
        else:
            out_refs[0][rows, :] = y


def ffn_ln(x, wu, wd, layer, g, b, slab_out, tm=1024):
    m = x.shape[0]
    row = pl.BlockSpec((tm, D_MODEL), lambda i: (i, 0))
    if slab_out:
        out_specs = [pl.BlockSpec((N_SLABS, tm, LANES), lambda i: (0, i, 0)), row]
        out_shape = [jax.ShapeDtypeStruct((N_SLABS, m, LANES), F32), jax.ShapeDtypeStruct((m, D_MODEL), BF16)]
    else:
        out_specs = row
        out_shape = jax.ShapeDtypeStruct((m, D_MODEL), F32)
    return pl.pallas_call(
        functools.partial(_ffn_kernel, slab_out=slab_out),
        grid=(m // tm,),
        in_specs=[row, _layer_of(wu, layer), _layer_of(wd, layer), _full((1, D_MODEL)), _full((1, D_MODEL))],
        out_specs=out_specs,
        out_shape=out_shape,
        compiler_params=_params(1),
        name="ffn_ln",
    )(x, wu, wd, g, b)


def _permute_kernel(x_ref, *out_refs, dils):
    for o_ref, d in zip(out_refs, dils):
        chunk = SPAN // d
        for r in range(d):
            for j in range(N_SLABS):
                o_ref[r * chunk:(r + 1) * chunk, j * LANES:(j + 1) * LANES] = (
                    x_ref[j, pl.ds(r, chunk, stride=d), :].astype(BF16))


def permute_tokens(x_slabs, dils):
    m = x_slabs.shape[1]
    return pl.pallas_call(
        functools.partial(_permute_kernel, dils=dils),
        grid=(m // SPAN,),
        in_specs=[pl.BlockSpec((N_SLABS, SPAN, LANES), lambda i: (0, i, 0))],
        out_specs=[pl.BlockSpec((SPAN, D_MODEL), lambda i: (i, 0)) for _ in dils],
        out_shape=[jax.ShapeDtypeStruct((m, D_MODEL), BF16) for _ in dils],
        compiler_params=_params(1),
        name="permute_tokens",
    )(x_slabs)


def _permute_table_kernel(cos_ref, sin_ref, *out_refs, dils):
    outs = iter(out_refs)
    for d in dils:
        chunk = SPAN // d
        for t_ref in (cos_ref, sin_ref):
            o_ref = next(outs)
            for r in range(d):
                o_ref[r * chunk:(r + 1) * chunk, :] = t_ref[pl.ds(r, chunk, stride=d), :]


def permute_tables(cos, sin_signed, dils):
    m = cos.shape[0]
    blk = pl.BlockSpec((SPAN, LANES), lambda i: (i, 0))
    outs = pl.pallas_call(
        functools.partial(_permute_table_kernel, dils=dils),
        grid=(m // SPAN,),
        in_specs=[blk, blk],
        out_specs=[blk] * (2 * len(dils)),
        out_shape=[jax.ShapeDtypeStruct((m, LANES), F32)] * (2 * len(dils)),
        compiler_params=_params(1),
        name="permute_tables",
    )(cos, sin_signed)
    return [(outs[2 * i], outs[2 * i + 1]) for i in range(len(dils))]


def _rope_slab(y, cos, sin_signed, first_half):
    partner = jnp.where(first_half, pltpu.roll(y, LANES - HEAD_DIM // 2, axis=1),
                        pltpu.roll(y, HEAD_DIM // 2, axis=1))
    return y * cos + partner * sin_signed


def _proj_kernel(*refs, n_w, plan, segs, t_width):
    x_ref, w_refs = refs[0], refs[1:1 + n_w]
    cos_ref, sin_ref = refs[1 + n_w:3 + n_w]
    out_refs = refs[3 + n_w:-2]
    w_scr, wt_scr = refs[-2:]

    @pl.when(pl.program_id(0) == 0)
    def _():
        for idx, lo, hi, transposed, dst in plan:
            blk = w_refs[idx][:, lo:hi]
            if transposed:
                wt_scr[dst:dst + hi - lo, :] = blk.T.astype(BF16)
            else:
                w_scr[:, dst:dst + hi - lo] = blk.astype(BF16)

    xb = x_ref[...]
    cos = cos_ref[...]
    sin_signed = sin_ref[...]
    lane = lax.broadcasted_iota(jnp.int32, cos.shape, 1)
    first_half = (lane % HEAD_DIM) < (HEAD_DIM // 2)
    col = 0
    for o_ref, (width, rope, scale) in zip(out_refs, segs):
        for c0 in range(0, width, MXU_COLS):
            cw = min(MXU_COLS, width - c0)
            y2 = jnp.dot(xb, w_scr[:, col + c0:col + c0 + cw], preferred_element_type=F32)
            for s0 in range(0, cw, LANES):
                y = y2[:, s0:s0 + LANES]
                if rope:
                    y = _rope_slab(y, cos, sin_signed, first_half)
                if scale != 1.0:
                    y = y * scale
                o_ref[:, c0 + s0:c0 + s0 + LANES] = y.astype(o_ref.dtype)
        col += width
    if t_width:
        t_ref = out_refs[len(segs)]
        for c0 in range(0, t_width, MXU_COLS):
            cw = min(MXU_COLS, t_width - c0)
            t_ref[c0:c0 + cw, :] = lax.dot_general(wt_scr[c0:c0 + cw, :], xb, NT,
                                                   preferred_element_type=F32).astype(t_ref.dtype)


def proj_rope(xb, w_in, layer, col0, block_w, t_width, cos, sin_signed, segs, name, tm=1024):
    m = xb.shape[0]
    n_row = sum(s[0] for s in segs)
    total = n_row + t_width
    assert col0 % block_w == 0 and total % block_w == 0
    n_w = total // block_w
    plan = []
    for idx in range(n_w):
        lo, hi = idx * block_w, (idx + 1) * block_w
        if lo < n_row:
            plan.append((idx, 0, min(hi, n_row) - lo, False, lo))
        if hi > n_row:
            start = max(lo, n_row)
            plan.append((idx, start - lo, block_w, True, start - n_row))
    w_specs = [pl.BlockSpec((None, D_MODEL, block_w), lambda i, cb=col0 // block_w + idx: (layer, 0, cb),
                            pipeline_mode=pl.Buffered(1)) for idx in range(n_w)]
    row = lambda wd: pl.BlockSpec((tm, wd), lambda i: (i, 0))
    out_specs = [row(s[0]) for s in segs]
    out_shape = [jax.ShapeDtypeStruct((m, s[0]), BF16) for s in segs]
    if t_width:
        out_specs.append(pl.BlockSpec((t_width, tm), lambda i: (0, i)))
        out_shape.append(jax.ShapeDtypeStruct((t_width, m), BF16))
    return pl.pallas_call(
        functools.partial(_proj_kernel, n_w=n_w, plan=tuple(plan), segs=segs, t_width=t_width),
        grid=(m // tm,),
        in_specs=[row(D_MODEL), *w_specs, row(LANES), row(LANES)],
        out_specs=out_specs,
        out_shape=out_shape,
        scratch_shapes=[pltpu.VMEM((D_MODEL, n_row), BF16), pltpu.VMEM((max(t_width, 16), D_MODEL), BF16)],
        compiler_params=_params(1),
        name=name,
    )(xb, *([w_in] * n_w), cos, sin_signed)


def _attn_kernel(*refs, n_heads, gqa, max_dist, dil, n_res, n_blocks, spans_per_batch, has_sink, want_lse):
    refs = list(refs)
    sink_ref = refs.pop(0) if has_sink else None
    n_halo = 1 if n_blocks == 1 else n_res
    q_ref, kc_ref, vc_ref = refs[:3]
    kh_refs, vh_refs = refs[3:3 + n_halo], refs[3 + n_halo:3 + 2 * n_halo]
    o_ref = refs[3 + 2 * n_halo]
    lse_ref = refs[4 + 2 * n_halo] if want_lse else None
    lse_scr, o_scr = refs[-2:]
    g = pl.program_id(0)
    r = pl.program_id(1)
    first_has_prev = (g % spans_per_batch) > 0
    key = lax.broadcasted_iota(jnp.int32, (2 * BLOCK, 2 * BLOCK), 0)
    qry = lax.broadcasted_iota(jnp.int32, (2 * BLOCK, 2 * BLOCK), 1) % BLOCK
    in_band_prev = jnp.logical_and(key < BLOCK, key >= qry + (BLOCK - max_dist))
    in_band_cur = jnp.logical_and(key >= BLOCK, key - BLOCK <= qry)
    bias_on = jnp.where(jnp.logical_or(in_band_prev, in_band_cur), 0.0, NEG_BIG)
    bias_first = jnp.where(jnp.logical_or(jnp.logical_and(in_band_prev, first_has_prev), in_band_cur),
                           0.0, NEG_BIG)
    low = lax.broadcasted_iota(jnp.int32, (BLOCK, LANES), 1) < HEAD_DIM
    lane2 = lax.broadcasted_iota(jnp.int32, (1, 2 * BLOCK), 1)
    ones_rows = jnp.ones((ONES_ROWS, 2 * BLOCK), BF16)
    n_pairs = n_heads // 2
    rep = n_heads // (kc_ref.shape[1] // HEAD_DIM)
    units = [(j, i, p) for j in range(n_res) for i in range(n_blocks) for p in range(n_pairs)]

    def rows_of(j, i):
        return slice((j * n_blocks + i) * BLOCK, (j * n_blocks + i + 1) * BLOCK)

    def prev_and_cur(j, i):
        return slice((j * n_blocks + i - 1) * BLOCK, (j * n_blocks + i + 1) * BLOCK)

    def halo_k(j, kcols):
        return kh_refs[0][j * BLOCK:(j + 1) * BLOCK, kcols] if n_halo == 1 else kh_refs[j][:, kcols]

    def halo_vt(j, vrows):
        return vh_refs[0][vrows, j * BLOCK:(j + 1) * BLOCK] if n_halo == 1 else vh_refs[j][vrows, :]

    def scores(u):
        j, i, p = units[u]
        rows = rows_of(j, i)
        qp = q_ref[rows, p * LANES:(p + 1) * LANES]
        zero = jnp.zeros_like(qp)
        if gqa:
            kv = (2 * p) // rep
            qr = pltpu.roll(qp, HEAD_DIM, axis=1)
            first, second = (qp, qr) if kv == 0 else (qr, qp)
            keep = low if kv == 0 else jnp.logical_not(low)
            rhs = jnp.concatenate([jnp.where(keep, first, zero), jnp.where(keep, second, zero)], axis=0)
            kcols = slice(0, LANES)
        else:
            rhs = jnp.concatenate([jnp.where(low, qp, zero), jnp.where(low, zero, qp)], axis=0)
            kcols = slice(p * LANES, (p + 1) * LANES)
        if i == 0:
            keys = jnp.concatenate([halo_k(j, kcols), kc_ref[rows, kcols]], axis=0)
            bias = bias_first
        else:
            keys, bias = kc_ref[prev_and_cur(j, i), kcols], bias_on
        return lax.dot_general(keys, rhs, NT, preferred_element_type=F32) + bias

    def softmax(u, s):
        p = units[u][2]
        m = jnp.max(s, axis=0, keepdims=True)
        sink_row = None
        if has_sink:
            sink_row = jnp.where(lane2 < BLOCK, sink_ref[2 * p], sink_ref[2 * p + 1]) * LOG2_E
            m = jnp.maximum(m, sink_row)
        return jnp.exp2(s - m).astype(BF16), m, sink_row

    def values(u, sm):
        j, i, p = units[u]
        vrows = (slice(((2 * p) // rep) * HEAD_DIM, ((2 * p) // rep + 1) * HEAD_DIM) if gqa
                 else slice(p * LANES, (p + 1) * LANES))
        if i == 0:
            vt = jnp.concatenate([halo_vt(j, vrows), vc_ref[vrows, rows_of(j, i)]], axis=1)
        else:
            vt = vc_ref[vrows, prev_and_cur(j, i)]
        return jnp.dot(jnp.concatenate([vt, ones_rows], axis=0), sm[0], preferred_element_type=F32)

    def store(u, res, sm):
        j, i, p = units[u]
        _, m, sink_row = sm
        n_feat = res.shape[0] - ONES_ROWS
        den = res[n_feat:n_feat + 1, :]
        if has_sink:
            den = den + jnp.exp2(sink_row - m)
        inv = 1.0 / den
        second = slice(0, HEAD_DIM) if gqa else slice(HEAD_DIM, 2 * HEAD_DIM)
        o_pair = jnp.concatenate([res[:HEAD_DIM, :BLOCK] * inv[:, :BLOCK],
                                  res[second, BLOCK:] * inv[:, BLOCK:]], axis=0)
        if dil == 1:
            dst = rows_of(j, i)
            o_ref[dst, p * LANES:(p + 1) * LANES] = o_pair.T.astype(o_ref.dtype)
        else:
            dst = pl.ds(r * n_res + j + dil * BLOCK * i, BLOCK, stride=dil)
            o_scr[p, dst, :] = o_pair.T
        if want_lse:
            lse = (m + jnp.log2(den)) * LN_2
            if p == 0:
                lse_scr[...] = jnp.zeros_like(lse_scr)
            lse_scr[2 * p:2 * p + 1, :] = lse[:, :BLOCK]
            lse_scr[2 * p + 1:2 * p + 2, :] = lse[:, BLOCK:]
            if p == n_pairs - 1:
                lse_ref[dst, :] = lse_scr[...].T

    _software_pipeline(len(units), scores, softmax, values, store, lags=ATTN_LAGS)

    if dil > 1:
        @pl.when(r == pl.num_programs(1) - 1)
        def _():
            for p in range(n_pairs):
                o_ref[:, p * LANES:(p + 1) * LANES] = o_scr[p].astype(o_ref.dtype)


def banded_attention(q, k, vt, max_dist, dil, seq, sink=None, want_lse=True):
    m, qw = q.shape
    kw = k.shape[1]
    n_heads = qw // HEAD_DIM
    gqa = kw != qw
    span = SPAN if dil > 1 else PLAIN_SPAN
    chunk = span // dil
    n_blocks = chunk // BLOCK
    n_res = max(ATTN_UNITS_BLOCKS // n_blocks, 1)
    spans_per_batch = seq // span
    steps = dil // n_res
    rows = chunk * n_res

    def cur(g, r):
        return (g * steps + r, 0)

    def cur_t(g, r):
        return (0, g * steps + r)

    def halo_block(g, r, j):
        return jnp.maximum(((g - 1) * dil + r * n_res + j + 1) * n_blocks - 1, 0)

    if n_blocks == 1:
        halo = BLOCK * n_res
        k_halos = [pl.BlockSpec((halo, kw), lambda g, r: (halo_block(g, r, 0) // n_res, 0))]
        v_halos = [pl.BlockSpec((kw, halo), lambda g, r: (0, halo_block(g, r, 0) // n_res))]
    else:
        k_halos = [pl.BlockSpec((BLOCK, kw), lambda g, r, j=j: (halo_block(g, r, j), 0)) for j in range(n_res)]
        v_halos = [pl.BlockSpec((kw, BLOCK), lambda g, r, j=j: (0, halo_block(g, r, j))) for j in range(n_res)]
    in_specs = [pl.BlockSpec((rows, qw), cur), pl.BlockSpec((rows, kw), cur), pl.BlockSpec((kw, rows), cur_t),
                *k_halos, *v_halos]
    args = [q, k, vt] + [k] * len(k_halos) + [vt] * len(v_halos)
    if sink is not None:
        in_specs = [pl.BlockSpec(memory_space=pltpu.SMEM)] + in_specs
        args = [sink] + args
    out_specs = [pl.BlockSpec((span, qw), lambda g, r: (g, 0))]
    out_shape = [jax.ShapeDtypeStruct((m, qw), BF16)]
    o_scr_rows = span if dil > 1 else 8
    if want_lse:
        out_specs.append(pl.BlockSpec((span, LANES), lambda g, r: (g, 0)))
        out_shape.append(jax.ShapeDtypeStruct((m, LANES), F32))
    return pl.pallas_call(
        functools.partial(_attn_kernel, n_heads=n_heads, gqa=gqa, max_dist=max_dist, dil=dil, n_res=n_res,
                          n_blocks=n_blocks, spans_per_batch=spans_per_batch,
                          has_sink=sink is not None, want_lse=want_lse),
        grid=(m // span, steps),
        in_specs=in_specs,
        out_specs=out_specs,
        out_shape=out_shape,
        scratch_shapes=[pltpu.VMEM((LANES, BLOCK), F32), pltpu.VMEM((qw // LANES, o_scr_rows, LANES), F32)],
        compiler_params=_params(2),
        name="banded_attention",
    )(*args)


def _retention_kernel(q_ref, k_ref, v_ref, gate_ref, o_ref, state_ref, dec_ref):
    n = pl.program_id(0)
    n_pairs = C_HEADS // 2
    log_g = [math.log1p(-2.0 ** (-5.0 - h)) for h in range(C_HEADS)]
    row = lax.broadcasted_iota(jnp.int32, (C_CHUNK, LANES), 0)
    lane = lax.broadcasted_iota(jnp.int32, (C_CHUNK, LANES), 1)
    low = lane < C_KEY_DIM

    @pl.when(n == 0)
    def _():
        state_ref[...] = jnp.zeros_like(state_ref)
        rel = (row - lane).astype(F32)
        idx = row.astype(F32)
        for h in range(C_HEADS):
            dec_ref[0, h] = jnp.where(rel >= 0, jnp.exp(log_g[h] * jnp.maximum(rel, 0.0)), 0.0)
            dec_ref[1, h] = jnp.exp(log_g[h] * (idx + 1.0))
        for p in range(n_pairs):
            lg = jnp.where(low, log_g[2 * p], log_g[2 * p + 1])
            dec_ref[2, p] = jnp.exp(lg * (C_CHUNK - 1.0 - idx))

    units = [(b, p) for b in range(q_ref.shape[0]) for p in range(n_pairs)]

    def scores(u):
        b, p = units[u]
        cols = slice(p * LANES, (p + 1) * LANES)
        qp, kp = q_ref[b, :, cols], k_ref[b, :, cols]
        zero = jnp.zeros_like(qp)
        lhs = jnp.concatenate([jnp.where(low, qp, zero), jnp.where(low, zero, qp)], axis=0)
        s = lax.dot_general(lhs, kp, NT, preferred_element_type=F32)
        qs = jnp.dot(lhs, state_ref[b, p].astype(BF16), preferred_element_type=F32)
        return s, qs, kp

    def decay(u, sc):
        _, p = units[u]
        s, qs, kp = sc
        a = [(s[hh * C_CHUNK:(hh + 1) * C_CHUNK] * dec_ref[0, 2 * p + hh]).astype(BF16) for hh in range(2)]
        kd_t = (kp.astype(F32) * dec_ref[2, p]).T.astype(BF16)
        return a, kd_t, qs

    def values(u, dc):
        b, p = units[u]
        a, kd_t, qs = dc
        outs = []
        for hh in range(2):
            h = 2 * p + hh
            v = v_ref[b, :, h * C_VAL_DIM:(h + 1) * C_VAL_DIM]
            srows = slice(hh * C_KEY_DIM, (hh + 1) * C_KEY_DIM)
            res = jnp.dot(jnp.concatenate([a[hh], kd_t[srows, :]], axis=0), v, preferred_element_type=F32)
            outs.append(res[:C_CHUNK] + qs[hh * C_CHUNK:(hh + 1) * C_CHUNK] * dec_ref[1, h])
            state_ref[b, p, srows, :] = (state_ref[b, p, srows, :] * math.exp(log_g[h] * C_CHUNK)
                                         + res[C_CHUNK:])
        return outs

    def store(u, outs):
        b, p = units[u]
        for hh in range(2):
            h = 2 * p + hh
            vs = slice(h * C_VAL_DIM, (h + 1) * C_VAL_DIM)
            o = outs[hh]
            mu = jnp.mean(o, axis=-1, keepdims=True)
            oc = o - mu
            var = jnp.mean(oc * oc, axis=-1, keepdims=True)
            gate = gate_ref[b, :, vs].astype(F32)
            o_ref[b, :, vs] = (gate * jax.nn.sigmoid(gate) * (oc * lax.rsqrt(var + LN_EPS))).astype(o_ref.dtype)

    _software_pipeline(len(units), scores, decay, values, lambda u, outs, _: store(u, outs), lags=RET_LAGS)


def retention_gated(q, k, v, gate):
    bsz, seq, _ = q.shape
    blk = lambda w: pl.BlockSpec((bsz, C_CHUNK, w), lambda n: (0, n, 0))
    return pl.pallas_call(
        _retention_kernel,
        grid=(seq // C_CHUNK,),
        in_specs=[blk(C_QK), blk(C_QK), blk(C_V), blk(C_V)],
        out_specs=blk(C_V),
        out_shape=jax.ShapeDtypeStruct((bsz, seq, C_V), BF16),
        scratch_shapes=[pltpu.VMEM((bsz, C_HEADS // 2, 2 * C_KEY_DIM, C_VAL_DIM), F32),
                        pltpu.VMEM((3, C_HEADS, C_CHUNK, LANES), F32)],
        compiler_params=_params(1),
        name="retention",
    )(q, k, v, gate)


def _merge_kernel(x_ref, xb_ref, o1_ref, o2_ref, o3_ref, l1_ref, l2_ref, l3_ref, yb_ref, yc_ref,
                  wg0_ref, wg1_ref, wg2_ref, gb_ref, pa_ref, pb_ref, pc_ref, wo_ref, e_ref, g_ref, b_ref,
                  out_ref, wg_ref):
    @pl.when(pl.program_id(0) == 0)
    def _():
        for j, blk_ref in enumerate((wg0_ref, wg1_ref, wg2_ref)):
            wg_ref[:, j * D_MODEL:(j + 1) * D_MODEL] = blk_ref[...].astype(BF16)

    expand = e_ref[...]
    lane = lax.broadcasted_iota(jnp.int32, (MERGE_SUB, LANES), 1)

    def per_lane(w):
        w = jnp.where(lane < A_HEADS, w, 0.0)
        hi = w.astype(BF16).astype(F32)
        packed = hi + pltpu.roll(w - hi, LANES // 2, axis=1)
        return jnp.dot(packed.astype(BF16), expand, preferred_element_type=F32)

    for sub in range(xb_ref.shape[0] // MERGE_SUB):
        rows = slice(sub * MERGE_SUB, (sub + 1) * MERGE_SUB)
        x = jnp.concatenate([x_ref[j, rows, :] for j in range(N_SLABS)], axis=1)
        xb = xb_ref[rows, :]
        l1, l2, l3 = l1_ref[rows, :], l2_ref[rows, :], l3_ref[rows, :]
        m = jnp.maximum(jnp.maximum(l1, l2), l3)
        e1, e2, e3 = jnp.exp(l1 - m), jnp.exp(l2 - m), jnp.exp(l3 - m)
        inv = 1.0 / (e1 + e2 + e3)
        o3 = o3_ref[rows, :].astype(F32)
        ya = (o3 + per_lane(e1 * inv) * (o1_ref[rows, :].astype(F32) - o3)
              + per_lane(e2 * inv) * (o2_ref[rows, :].astype(F32) - o3))

        def gate(j, xb=xb):
            cols = slice(j * D_MODEL, (j + 1) * D_MODEL)
            return jax.nn.sigmoid(jnp.dot(xb, wg_ref[:, cols], preferred_element_type=F32) + gb_ref[:, cols])

        merged = gate(0) * jnp.dot(ya.astype(BF16), pa_ref[...], preferred_element_type=F32)
        merged += gate(1) * jnp.dot(yb_ref[rows, :], pb_ref[...], preferred_element_type=F32)
        merged += gate(2) * jnp.dot(yc_ref[rows, :], pc_ref[...], preferred_element_type=F32)
        mix = jnp.dot(merged.astype(BF16), wo_ref[...], preferred_element_type=F32)
        out_ref[rows, :] = _layer_norm(ALPHA * x + mix, g_ref[...], b_ref[...])


def merge_ln(x_slabs, xb, o1, o2, o3, l1, l2, l3, yb, yc, w_in, layer, gate_col0, gb, pa, pb, pc, wo, g, b,
             tm=512):
    m = xb.shape[0]
    assert gate_col0 % D_MODEL == 0
    gate_specs = [pl.BlockSpec((None, D_MODEL, D_MODEL), lambda i, cb=gate_col0 // D_MODEL + j: (layer, 0, cb),
                               pipeline_mode=pl.Buffered(1)) for j in range(3)]
    head = jnp.arange(A_W) // HEAD_DIM
    expand = ((jnp.arange(LANES)[:, None] % (LANES // 2)) == head[None, :]).astype(BF16)
    row = lambda w: pl.BlockSpec((tm, w), lambda i: (i, 0))
    return pl.pallas_call(
        _merge_kernel,
        grid=(m // tm,),
        in_specs=[pl.BlockSpec((N_SLABS, tm, LANES), lambda i: (0, i, 0)), row(D_MODEL),
                  row(A_W), row(A_W), row(A_W), row(LANES), row(LANES), row(LANES), row(B_QW), row(C_V),
                  *gate_specs, _full(gb.shape), _layer_of(pa, layer),
                  _layer_of(pb, layer), _layer_of(pc, layer), _layer_of(wo, layer), _full(expand.shape),
                  _full((1, D_MODEL)), _full((1, D_MODEL))],
        out_specs=row(D_MODEL),
        out_shape=jax.ShapeDtypeStruct((m, D_MODEL), F32),
        scratch_shapes=[pltpu.VMEM((D_MODEL, 3 * D_MODEL), BF16)],
        compiler_params=_params(1),
        name="merge_ln",
    )(x_slabs, xb, o1, o2, o3, l1, l2, l3, yb, yc, w_in, w_in, w_in, gb, pa, pb, pc, wo, expand, g, b)


def kernel(x, positions, w_in, gate_bias, attn_sinks, w_proj_a, w_proj_b, w_proj_c, w_out, ffn1_up, ffn1_down, ffn2_up, ffn2_down, ln1_g, ln1_b, ln2_g, ln2_b, ln3_g, ln3_b):
    bsz, seq, _ = x.shape
    m = bsz * seq
    dils = tuple(d for _, d in A_GROUPS)
    tables = {1: rope_tables(positions.reshape(m, 1))}
    tables.update(zip(dils[1:], permute_tables(*tables[1], dils[1:])))
    x = x.reshape(m, D_MODEL)
    a_end = 9 * A_W
    b_end = a_end + B_QW + 2 * B_KVW
    c_end = b_end + 2 * C_QK + 2 * C_V
    qk_scale = HEAD_DIM ** -0.5 * LOG2_E
    seg_a = ((A_W, True, qk_scale), (A_W, True, 1.0))
    seg_b = ((B_QW, True, qk_scale), (B_KVW, True, 1.0))
    seg_c = ((C_QK, True, 1.0), (C_QK, True, C_KEY_DIM ** -0.5), (C_V, False, 1.0), (C_V, False, 1.0))
    row = lambda t: t.reshape(1, -1)
    ffn1 = (ffn1_up.astype(BF16), ffn1_down.astype(BF16))
    ffn2 = (ffn2_up.astype(BF16), ffn2_down.astype(BF16))
    branch_w = tuple(t.astype(BF16) for t in (w_proj_a, w_proj_b, w_proj_c, w_out))
    for l in range(DEPTH):
        x_slabs, xb = ffn_ln(x, *ffn1, l, row(ln1_g[l]), row(ln1_b[l]), True)
        xb_by_dil = dict(zip(dils[1:], permute_tokens(x_slabs, dils[1:])))
        xb_by_dil[1] = xb
        outs, lses = [], []
        for gi, (window, dil) in enumerate(A_GROUPS):
            q, k, vt = proj_rope(xb_by_dil[dil], w_in, l, 3 * gi * A_W, A_W, A_W, *tables[dil], seg_a, "proj_a")
            o, lse = banded_attention(q, k, vt, window // dil, dil, seq)
            outs.append(o)
            lses.append(lse)
        qb, kb, vtb = proj_rope(xb, w_in, l, a_end, MXU_COLS, B_KVW, *tables[1], seg_b, "proj_b")
        yb, = banded_attention(qb, kb, vtb, B_WINDOW - 1, 1, seq, attn_sinks[l], want_lse=False)
        qc, kc, vc, gc = proj_rope(xb, w_in, l, b_end, D_MODEL, 0, *tables[1], seg_c, "proj_c")
        yc = retention_gated(qc.reshape(bsz, seq, C_QK), kc.reshape(bsz, seq, C_QK),
                             vc.reshape(bsz, seq, C_V), gc.reshape(bsz, seq, C_V))
        x = merge_ln(x_slabs, xb, *outs, *lses, yb, yc.reshape(m, C_V),
                     w_in, l, c_end, row(gate_bias[l]), *branch_w, row(ln2_g[l]), row(ln2_b[l]))
        x = ffn_ln(x, *ffn2, l, row(ln3_g[l]), row(ln3_b[l]), False)
    return x.reshape(bsz, seq, D_MODEL)
```

```python
import functools
import math

import jax
import jax.numpy as jnp
from jax import lax
from jax.experimental import pallas as pl
from jax.experimental.pallas import tpu as pltpu

D_MODEL = 1024
DEPTH = 2
HEAD_DIM = 64
BLOCK = 128
A_GROUPS = ((128, 1), (512, 4), (2048, 16))
A_HEADS = 12
A_W = A_HEADS * HEAD_DIM
B_Q_HEADS = 16
B_KV_HEADS = 2
B_WINDOW = 128
B_QW = B_Q_HEADS * HEAD_DIM
B_KVW = B_KV_HEADS * HEAD_DIM
C_HEADS = 8
C_KEY_DIM = 64
C_VAL_DIM = 128
C_CHUNK = 128
C_QK = C_HEADS * C_KEY_DIM
C_V = C_HEADS * C_VAL_DIM
D_FF = 2816
ROPE_THETA = 10000.0
LN_EPS = 1e-5
ALPHA = (2.0 * DEPTH) ** 0.25

LANES = 128
MXU_COLS = 256
FF_CHUNK = 256
FFN_SUB = 512
MERGE_SUB = 256
PROJ_ROWS_WIDE = 2048
SPAN = 2048
PLAIN_SPAN = 1024
ATTN_UNITS_BLOCKS = 8
ATTN_LAGS = (3, 4, 7)
RET_LAGS = (1, 2, 3)
NEG_BIG = -1e30
ONES_ROWS = 16
LOG2_E = math.log2(math.e)
LN_2 = math.log(2.0)
VMEM_LIMIT = 56 * 1024 * 1024
N_SLABS = D_MODEL // LANES

F32 = jnp.float32
BF16 = jnp.bfloat16
NT = (((1,), (1,)), ((), ()))


def _params(n_axes):
    return pltpu.CompilerParams(dimension_semantics=("arbitrary",) * n_axes,
                                vmem_limit_bytes=VMEM_LIMIT)


def _layer_norm(y, g, b):
    mu = jnp.mean(y, axis=-1, keepdims=True)
    yc = y - mu
    var = jnp.mean(yc * yc, axis=-1, keepdims=True)
    return yc * lax.rsqrt(var + LN_EPS) * g + b


def _full(shape):
    nd = len(shape)
    return pl.BlockSpec(shape, lambda *_: (0,) * nd, pipeline_mode=pl.Buffered(1))


def _layer_of(stacked, layer):
    nd = stacked.ndim - 1
    return pl.BlockSpec((None,) + stacked.shape[1:], lambda *_: (layer,) + (0,) * nd,
                        pipeline_mode=pl.Buffered(1))


def _software_pipeline(n, stage_a, stage_b, stage_c, stage_d, lags=(1, 2, 3)):
    a_val, b_val, c_val = {}, {}, {}
    lag_b, lag_c, lag_d = lags
    for t in range(n + lag_d):
        if t < n:
            a_val[t] = stage_a(t)
        if 0 <= t - lag_b < n:
            b_val[t - lag_b] = stage_b(t - lag_b, a_val.pop(t - lag_b))
        if 0 <= t - lag_c < n:
            c_val[t - lag_c] = stage_c(t - lag_c, b_val[t - lag_c])
        if 0 <= t - lag_d < n:
            stage_d(t - lag_d, c_val.pop(t - lag_d), b_val.pop(t - lag_d))


def _rope_table_kernel(pos_ref, inv_ref, sign_ref, cos_ref, sin_ref):
    ang = pos_ref[...].astype(F32) * inv_ref[...]
    cos_ref[...] = jnp.cos(ang)
    sin_ref[...] = jnp.sin(ang) * sign_ref[...]


def rope_tables(pos_col, tm=1024):
    m = pos_col.shape[0]
    half = HEAD_DIM // 2
    inv = ROPE_THETA ** (-jnp.arange(half, dtype=F32) / half)
    inv_row = jnp.tile(inv, LANES // half)[None, :]
    lane = jnp.arange(LANES)
    sign_row = jnp.where(lane % HEAD_DIM < half, -1.0, 1.0).astype(F32)[None, :]
    return pl.pallas_call(
        _rope_table_kernel,
        grid=(m // tm,),
        in_specs=[pl.BlockSpec((tm, 1), lambda i: (i, 0)), _full((1, LANES)), _full((1, LANES))],
        out_specs=[pl.BlockSpec((tm, LANES), lambda i: (i, 0))] * 2,
        out_shape=[jax.ShapeDtypeStruct((m, LANES), F32)] * 2,
        compiler_params=_params(1),
        name="rope_table",
    )(pos_col, inv_row, sign_row)


def _ffn_kernel(x_ref, wu_ref, wd_ref, g_ref, b_ref, *out_refs, slab_out):
    n_chunks = D_FF // FF_CHUNK
    for sub in range(x_ref.shape[0] // FFN_SUB):
        rows = slice(sub * FFN_SUB, (sub + 1) * FFN_SUB)
        x = x_ref[rows, :]
        xb = x.astype(BF16)

        def hidden(c, xb=xb):
            lo = c * FF_CHUNK
            a = jnp.dot(xb, wu_ref[:, lo:lo + FF_CHUNK], preferred_element_type=F32)
            b = jnp.dot(xb, wu_ref[:, D_FF + lo:D_FF + lo + FF_CHUNK], preferred_element_type=F32)
            return (a * jax.nn.sigmoid(a) * b).astype(BF16)

        h = hidden(0)
        acc = None
        for c in range(n_chunks):
            h_next = hidden(c + 1) if c + 1 < n_chunks else None
            d = jnp.dot(h, wd_ref[c * FF_CHUNK:(c + 1) * FF_CHUNK, :], preferred_element_type=F32)
            acc = d if acc is None else acc + d
            h = h_next
        y = _layer_norm(ALPHA * x + 0.5 * acc, g_ref[...], b_ref[...])
        if slab_out:
            slab_ref, bf_ref = out_refs
            for j in range(N_SLABS):
                slab_ref[j, rows, :] = y[:, j * LANES:(j + 1) * LANES]
            bf_ref[rows, :] = y.astype(BF16)
        else:
            out_refs[0][rows, :] = y


def ffn_ln(x, wu, wd, layer, g, b, slab_out, tm=1024):
    m = x.shape[0]
    row = pl.BlockSpec((tm, D_MODEL), lambda i: (i, 0))
    if slab_out:
        out_specs = [pl.BlockSpec((N_SLABS, tm, LANES), lambda i: (0, i, 0)), row]
        out_shape = [jax.ShapeDtypeStruct((N_SLABS, m, LANES), F32), jax.ShapeDtypeStruct((m, D_MODEL), BF16)]
    else:
        out_specs = row
        out_shape = jax.ShapeDtypeStruct((m, D_MODEL), F32)
    return pl.pallas_call(
        functools.partial(_ffn_kernel, slab_out=slab_out),
        grid=(m // tm,),
        in_specs=[row, _layer_of(wu, layer), _layer_of(wd, layer), _full((1, D_MODEL)), _full((1, D_MODEL))],
        out_specs=out_specs,
        out_shape=out_shape,
        compiler_params=_params(1),
        name="ffn_ln",
    )(x, wu, wd, g, b)


def _permute_kernel(x_ref, *out_refs, dils):
    for o_ref, d in zip(out_refs, dils):
        chunk = SPAN // d
        for r in range(d):
            for j in range(N_SLABS):
                o_ref[r * chunk:(r + 1) * chunk, j * LANES:(j + 1) * LANES] = (
                    x_ref[j, pl.ds(r, chunk, stride=d), :].astype(BF16))


def permute_tokens(x_slabs, dils):
    m = x_slabs.shape[1]
    return pl.pallas_call(
        functools.partial(_permute_kernel, dils=dils),
        grid=(m // SPAN,),
        in_specs=[pl.BlockSpec((N_SLABS, SPAN, LANES), lambda i: (0, i, 0))],
        out_specs=[pl.BlockSpec((SPAN, D_MODEL), lambda i: (i, 0)) for _ in dils],
        out_shape=[jax.ShapeDtypeStruct((m, D_MODEL), BF16) for _ in dils],
        compiler_params=_params(1),
        name="permute_tokens",
    )(x_slabs)


def _permute_table_kernel(cos_ref, sin_ref, *out_refs, dils):
    outs = iter(out_refs)
    for d in dils:
        chunk = SPAN // d
        for t_ref in (cos_ref, sin_ref):
            o_ref = next(outs)
            for r in range(d):
                o_ref[r * chunk:(r + 1) * chunk, :] = t_ref[pl.ds(r, chunk, stride=d), :]


def permute_tables(cos, sin_signed, dils):
    m = cos.shape[0]
    blk = pl.BlockSpec((SPAN, LANES), lambda i: (i, 0))
    outs = pl.pallas_call(
        functools.partial(_permute_table_kernel, dils=dils),
        grid=(m // SPAN,),
        in_specs=[blk, blk],
        out_specs=[blk] * (2 * len(dils)),
        out_shape=[jax.ShapeDtypeStruct((m, LANES), F32)] * (2 * len(dils)),
        compiler_params=_params(1),
        name="permute_tables",
    )(cos, sin_signed)
    return [(outs[2 * i], outs[2 * i + 1]) for i in range(len(dils))]


def _rope_slab(y, cos, sin_signed, first_half):
    partner = jnp.where(first_half, pltpu.roll(y, LANES - HEAD_DIM // 2, axis=1),
                        pltpu.roll(y, HEAD_DIM // 2, axis=1))
    return y * cos + partner * sin_signed


def _proj_kernel(*refs, n_w, plan, segs, t_width):
    x_ref, w_refs = refs[0], refs[1:1 + n_w]
    cos_ref, sin_ref = refs[1 + n_w:3 + n_w]
    out_refs = refs[3 + n_w:-2]
    w_scr, wt_scr = refs[-2:]

    @pl.when(pl.program_id(0) == 0)
    def _():
        for idx, lo, hi, transposed, dst in plan:
            blk = w_refs[idx][:, lo:hi]
            if transposed:
                wt_scr[dst:dst + hi - lo, :] = blk.T.astype(BF16)
            else:
                w_scr[:, dst:dst + hi - lo] = blk.astype(BF16)

    xb = x_ref[...]
    cos = cos_ref[...]
    sin_signed = sin_ref[...]
    lane = lax.broadcasted_iota(jnp.int32, cos.shape, 1)
    first_half = (lane % HEAD_DIM) < (HEAD_DIM // 2)
    col = 0
    for o_ref, (width, rope, scale) in zip(out_refs, segs):
        for c0 in range(0, width, MXU_COLS):
            cw = min(MXU_COLS, width - c0)
            y2 = jnp.dot(xb, w_scr[:, col + c0:col + c0 + cw], preferred_element_type=F32)
            for s0 in range(0, cw, LANES):
                y = y2[:, s0:s0 + LANES]
                if rope:
                    y = _rope_slab(y, cos, sin_signed, first_half)
                if scale != 1.0:
                    y = y * scale
                o_ref[:, c0 + s0:c0 + s0 + LANES] = y.astype(o_ref.dtype)
        col += width
    if t_width:
        t_ref = out_refs[len(segs)]
        for c0 in range(0, t_width, MXU_COLS):
            cw = min(MXU_COLS, t_width - c0)
            t_ref[c0:c0 + cw, :] = lax.dot_general(wt_scr[c0:c0 + cw, :], xb, NT,
                                                   preferred_element_type=F32).astype(t_ref.dtype)


def proj_rope(xb, w_in, layer, col0, block_w, t_width, cos, sin_signed, segs, name, tm=1024):
    m = xb.shape[0]
    n_row = sum(s[0] for s in segs)
    total = n_row + t_width
    assert col0 % block_w == 0 and total % block_w == 0
    n_w = total // block_w
    plan = []
    for idx in range(n_w):
        lo, hi = idx * block_w, (idx + 1) * block_w
        if lo < n_row:
            plan.append((idx, 0, min(hi, n_row) - lo, False, lo))
        if hi > n_row:
            start = max(lo, n_row)
            plan.append((idx, start - lo, block_w, True, start - n_row))
    w_specs = [pl.BlockSpec((None, D_MODEL, block_w), lambda i, cb=col0 // block_w + idx: (layer, 0, cb),
                            pipeline_mode=pl.Buffered(1)) for idx in range(n_w)]
    row = lambda wd: pl.BlockSpec((tm, wd), lambda i: (i, 0))
    out_specs = [row(s[0]) for s in segs]
    out_shape = [jax.ShapeDtypeStruct((m, s[0]), BF16) for s in segs]
    if t_width:
        out_specs.append(pl.BlockSpec((t_width, tm), lambda i: (0, i)))
        out_shape.append(jax.ShapeDtypeStruct((t_width, m), BF16))
    return pl.pallas_call(
        functools.partial(_proj_kernel, n_w=n_w, plan=tuple(plan), segs=segs, t_width=t_width),
        grid=(m // tm,),
        in_specs=[row(D_MODEL), *w_specs, row(LANES), row(LANES)],
        out_specs=out_specs,
        out_shape=out_shape,
        scratch_shapes=[pltpu.VMEM((D_MODEL, n_row), BF16), pltpu.VMEM((max(t_width, 16), D_MODEL), BF16)],
        compiler_params=_params(1),
        name=name,
    )(xb, *([w_in] * n_w), cos, sin_signed)


def _attn_kernel(*refs, n_heads, gqa, max_dist, dil, n_res, n_blocks, spans_per_batch, has_sink, want_lse):
    refs = list(refs)
    sink_ref = refs.pop(0) if has_sink else None
    n_halo = 1 if n_blocks == 1 else n_res
    q_ref, kc_ref, vc_ref = refs[:3]
    kh_refs, vh_refs = refs[3:3 + n_halo], refs[3 + n_halo:3 + 2 * n_halo]
    o_ref = refs[3 + 2 * n_halo]
    lse_ref = refs[4 + 2 * n_halo] if want_lse else None
    lse_scr, o_scr = refs[-2:]
    g = pl.program_id(0)
    r = pl.program_id(1)
    first_has_prev = (g % spans_per_batch) > 0
    key = lax.broadcasted_iota(jnp.int32, (2 * BLOCK, 2 * BLOCK), 0)
    qry = lax.broadcasted_iota(jnp.int32, (2 * BLOCK, 2 * BLOCK), 1) % BLOCK
    in_band_prev = jnp.logical_and(key < BLOCK, key >= qry + (BLOCK - max_dist))
    in_band_cur = jnp.logical_and(key >= BLOCK, key - BLOCK <= qry)
    bias_on = jnp.where(jnp.logical_or(in_band_prev, in_band_cur), 0.0, NEG_BIG)
    bias_first = jnp.where(jnp.logical_or(jnp.logical_and(in_band_prev, first_has_prev), in_band_cur),
                           0.0, NEG_BIG)
    low = lax.broadcasted_iota(jnp.int32, (BLOCK, LANES), 1) < HEAD_DIM
    lane2 = lax.broadcasted_iota(jnp.int32, (1, 2 * BLOCK), 1)
    ones_rows = jnp.ones((ONES_ROWS, 2 * BLOCK), BF16)
    n_pairs = n_heads // 2
    rep = n_heads // (kc_ref.shape[1] // HEAD_DIM)
    units = [(j, i, p) for j in range(n_res) for i in range(n_blocks) for p in range(n_pairs)]

    def rows_of(j, i):
        return slice((j * n_blocks + i) * BLOCK, (j * n_blocks + i + 1) * BLOCK)

    def prev_and_cur(j, i):
        return slice((j * n_blocks + i - 1) * BLOCK, (j * n_blocks + i + 1) * BLOCK)

    def halo_k(j, kcols):
        return kh_refs[0][j * BLOCK:(j + 1) * BLOCK, kcols] if n_halo == 1 else kh_refs[j][:, kcols]

    def halo_vt(j, vrows):
        return vh_refs[0][vrows, j * BLOCK:(j + 1) * BLOCK] if n_halo == 1 else vh_refs[j][vrows, :]

    def scores(u):
        j, i, p = units[u]
        rows = rows_of(j, i)
        qp = q_ref[rows, p * LANES:(p + 1) * LANES]
        zero = jnp.zeros_like(qp)
        if gqa:
            kv = (2 * p) // rep
            qr = pltpu.roll(qp, HEAD_DIM, axis=1)
            first, second = (qp, qr) if kv == 0 else (qr, qp)
            keep = low if kv == 0 else jnp.logical_not(low)
            rhs = jnp.concatenate([jnp.where(keep, first, zero), jnp.where(keep, second, zero)], axis=0)
            kcols = slice(0, LANES)
        else:
            rhs = jnp.concatenate([jnp.where(low, qp, zero), jnp.where(low, zero, qp)], axis=0)
            kcols = slice(p * LANES, (p + 1) * LANES)
        if i == 0:
            keys = jnp.concatenate([halo_k(j, kcols), kc_ref[rows, kcols]], axis=0)
            bias = bias_first
        else:
            keys, bias = kc_ref[prev_and_cur(j, i), kcols], bias_on
        return lax.dot_general(keys, rhs, NT, preferred_element_type=F32) + bias

    def softmax(u, s):
        p = units[u][2]
        m = jnp.max(s, axis=0, keepdims=True)
        sink_row = None
        if has_sink:
            sink_row = jnp.where(lane2 < BLOCK, sink_ref[2 * p], sink_ref[2 * p + 1]) * LOG2_E
            m = jnp.maximum(m, sink_row)
        return jnp.exp2(s - m).astype(BF16), m, sink_row

    def values(u, sm):
        j, i, p = units[u]
        vrows = (slice(((2 * p) // rep) * HEAD_DIM, ((2 * p) // rep + 1) * HEAD_DIM) if gqa
                 else slice(p * LANES, (p + 1) * LANES))
        if i == 0:
            vt = jnp.concatenate([halo_vt(j, vrows), vc_ref[vrows, rows_of(j, i)]], axis=1)
        else:
            vt = vc_ref[vrows, prev_and_cur(j, i)]
        return jnp.dot(jnp.concatenate([vt, ones_rows], axis=0), sm[0], preferred_element_type=F32)

    def store(u, res, sm):
        j, i, p = units[u]
        _, m, sink_row = sm
        n_feat = res.shape[0] - ONES_ROWS
        den = res[n_feat:n_feat + 1, :]
        if has_sink:
            den = den + jnp.exp2(sink_row - m)
        inv = 1.0 / den
        second = slice(0, HEAD_DIM) if gqa else slice(HEAD_DIM, 2 * HEAD_DIM)
        o_pair = jnp.concatenate([res[:HEAD_DIM, :BLOCK] * inv[:, :BLOCK],
                                  res[second, BLOCK:] * inv[:, BLOCK:]], axis=0)
        if dil == 1:
            dst = rows_of(j, i)
            o_ref[dst, p * LANES:(p + 1) * LANES] = o_pair.T.astype(o_ref.dtype)
        else:
            dst = pl.ds(r * n_res + j + dil * BLOCK * i, BLOCK, stride=dil)
            o_scr[p, dst, :] = o_pair.T
        if want_lse:
            lse = (m + jnp.log2(den)) * LN_2
            if p == 0:
                lse_scr[...] = jnp.zeros_like(lse_scr)
            lse_scr[2 * p:2 * p + 1, :] = lse[:, :BLOCK]
            lse_scr[2 * p + 1:2 * p + 2, :] = lse[:, BLOCK:]
            if p == n_pairs - 1:
                lse_ref[dst, :] = lse_scr[...].T

    _software_pipeline(len(units), scores, softmax, values, store, lags=ATTN_LAGS)

    if dil > 1:
        @pl.when(r == pl.num_programs(1) - 1)
        def _():
            for p in range(n_pairs):
                o_ref[:, p * LANES:(p + 1) * LANES] = o_scr[p].astype(o_ref.dtype)


def banded_attention(q, k, vt, max_dist, dil, seq, sink=None, want_lse=True):
    m, qw = q.shape
    kw = k.shape[1]
    n_heads = qw // HEAD_DIM
    gqa = kw != qw
    span = SPAN if dil > 1 else PLAIN_SPAN
    chunk = span // dil
    n_blocks = chunk // BLOCK
    n_res = max(ATTN_UNITS_BLOCKS // n_blocks, 1)
    spans_per_batch = seq // span
    steps = dil // n_res
    rows = chunk * n_res

    def cur(g, r):
        return (g * steps + r, 0)

    def cur_t(g, r):
        return (0, g * steps + r)

    def halo_block(g, r, j):
        return jnp.maximum(((g - 1) * dil + r * n_res + j + 1) * n_blocks - 1, 0)

    if n_blocks == 1:
        halo = BLOCK * n_res
        k_halos = [pl.BlockSpec((halo, kw), lambda g, r: (halo_block(g, r, 0) // n_res, 0))]
        v_halos = [pl.BlockSpec((kw, halo), lambda g, r: (0, halo_block(g, r, 0) // n_res))]
    else:
        k_halos = [pl.BlockSpec((BLOCK, kw), lambda g, r, j=j: (halo_block(g, r, j), 0)) for j in range(n_res)]
        v_halos = [pl.BlockSpec((kw, BLOCK), lambda g, r, j=j: (0, halo_block(g, r, j))) for j in range(n_res)]
    in_specs = [pl.BlockSpec((rows, qw), cur), pl.BlockSpec((rows, kw), cur), pl.BlockSpec((kw, rows), cur_t),
                *k_halos, *v_halos]
    args = [q, k, vt] + [k] * len(k_halos) + [vt] * len(v_halos)
    if sink is not None:
        in_specs = [pl.BlockSpec(memory_space=pltpu.SMEM)] + in_specs
        args = [sink] + args
    out_specs = [pl.BlockSpec((span, qw), lambda g, r: (g, 0))]
    out_shape = [jax.ShapeDtypeStruct((m, qw), BF16)]
    o_scr_rows = span if dil > 1 else 8
    if want_lse:
        out_specs.append(pl.BlockSpec((span, LANES), lambda g, r: (g, 0)))
        out_shape.append(jax.ShapeDtypeStruct((m, LANES), F32))
    return pl.pallas_call(
        functools.partial(_attn_kernel, n_heads=n_heads, gqa=gqa, max_dist=max_dist, dil=dil, n_res=n_res,
                          n_blocks=n_blocks, spans_per_batch=spans_per_batch,
                          has_sink=sink is not None, want_lse=want_lse),
        grid=(m // span, steps),
        in_specs=in_specs,
        out_specs=out_specs,
        out_shape=out_shape,
        scratch_shapes=[pltpu.VMEM((LANES, BLOCK), F32), pltpu.VMEM((qw // LANES, o_scr_rows, LANES), F32)],
        compiler_params=_params(2),
        name="banded_attention",
    )(*args)


def _retention_kernel(q_ref, k_ref, v_ref, gate_ref, o_ref, state_ref, dec_ref):
    n = pl.program_id(0)
    n_pairs = C_HEADS // 2
    log_g = [math.log1p(-2.0 ** (-5.0 - h)) for h in range(C_HEADS)]
    row = lax.broadcasted_iota(jnp.int32, (C_CHUNK, LANES), 0)
    lane = lax.broadcasted_iota(jnp.int32, (C_CHUNK, LANES), 1)
    low = lane < C_KEY_DIM

    @pl.when(n == 0)
    def _():
        state_ref[...] = jnp.zeros_like(state_ref)
        rel = (row - lane).astype(F32)
        idx = row.astype(F32)
        for h in range(C_HEADS):
            dec_ref[0, h] = jnp.where(rel >= 0, jnp.exp(log_g[h] * jnp.maximum(rel, 0.0)), 0.0)
            dec_ref[1, h] = jnp.exp(log_g[h] * (idx + 1.0))
        for p in range(n_pairs):
            lg = jnp.where(low, log_g[2 * p], log_g[2 * p + 1])
            dec_ref[2, p] = jnp.exp(lg * (C_CHUNK - 1.0 - idx))

    units = [(b, p) for b in range(q_ref.shape[0]) for p in range(n_pairs)]

    def scores(u):
        b, p = units[u]
        cols = slice(p * LANES, (p + 1) * LANES)
        qp, kp = q_ref[b, :, cols], k_ref[b, :, cols]
        zero = jnp.zeros_like(qp)
        lhs = jnp.concatenate([jnp.where(low, qp, zero), jnp.where(low, zero, qp)], axis=0)
        s = lax.dot_general(lhs, kp, NT, preferred_element_type=F32)
        qs = jnp.dot(lhs, state_ref[b, p].astype(BF16), preferred_element_type=F32)
        return s, qs, kp

    def decay(u, sc):
        _, p = units[u]
        s, qs, kp = sc
        a = [(s[hh * C_CHUNK:(hh + 1) * C_CHUNK] * dec_ref[0, 2 * p + hh]).astype(BF16) for hh in range(2)]
        kd_t = (kp.astype(F32) * dec_ref[2, p]).T.astype(BF16)
        return a, kd_t, qs

    def values(u, dc):
        b, p = units[u]
        a, kd_t, qs = dc
        outs = []
        for hh in range(2):
            h = 2 * p + hh
            v = v_ref[b, :, h * C_VAL_DIM:(h + 1) * C_VAL_DIM]
            srows = slice(hh * C_KEY_DIM, (hh + 1) * C_KEY_DIM)
            res = jnp.dot(jnp.concatenate([a[hh], kd_t[srows, :]], axis=0), v, preferred_element_type=F32)
            outs.append(res[:C_CHUNK] + qs[hh * C_CHUNK:(hh + 1) * C_CHUNK] * dec_ref[1, h])
            state_ref[b, p, srows, :] = (state_ref[b, p, srows, :] * math.exp(log_g[h] * C_CHUNK)
                                         + res[C_CHUNK:])
        return outs

    def store(u, outs):
        b, p = units[u]
        for hh in range(2):
            h = 2 * p + hh
            vs = slice(h * C_VAL_DIM, (h + 1) * C_VAL_DIM)
            o = outs[hh]
            mu = jnp.mean(o, axis=-1, keepdims=True)
            oc = o - mu
            var = jnp.mean(oc * oc, axis=-1, keepdims=True)
            gate = gate_ref[b, :, vs].astype(F32)
            o_ref[b, :, vs] = (gate * jax.nn.sigmoid(gate) * (oc * lax.rsqrt(var + LN_EPS))).astype(o_ref.dtype)

    _software_pipeline(len(units), scores, decay, values, lambda u, outs, _: store(u, outs), lags=RET_LAGS)


def retention_gated(q, k, v, gate):
    bsz, seq, _ = q.shape
    blk = lambda w: pl.BlockSpec((bsz, C_CHUNK, w), lambda n: (0, n, 0))
    return pl.pallas_call(
        _retention_kernel,
        grid=(seq // C_CHUNK,),
        in_specs=[blk(C_QK), blk(C_QK), blk(C_V), blk(C_V)],
        out_specs=blk(C_V),
        out_shape=jax.ShapeDtypeStruct((bsz, seq, C_V), BF16),
        scratch_shapes=[pltpu.VMEM((bsz, C_HEADS // 2, 2 * C_KEY_DIM, C_VAL_DIM), F32),
                        pltpu.VMEM((3, C_HEADS, C_CHUNK, LANES), F32)],
        compiler_params=_params(1),
        name="retention",
    )(q, k, v, gate)


def _merge_kernel(x_ref, xb_ref, o1_ref, o2_ref, o3_ref, l1_ref, l2_ref, l3_ref, yb_ref, yc_ref,
                  wg0_ref, wg1_ref, wg2_ref, gb_ref, pa_ref, pb_ref, pc_ref, wo_ref, e_ref, g_ref, b_ref,
                  out_ref, wg_ref):
    @pl.when(pl.program_id(0) == 0)
    def _():
        for j, blk_ref in enumerate((wg0_ref, wg1_ref, wg2_ref)):
            wg_ref[:, j * D_MODEL:(j + 1) * D_MODEL] = blk_ref[...].astype(BF16)

    expand = e_ref[...]
    lane = lax.broadcasted_iota(jnp.int32, (MERGE_SUB, LANES), 1)

    def per_lane(w):
        w = jnp.where(lane < A_HEADS, w, 0.0)
        hi = w.astype(BF16).astype(F32)
        packed = hi + pltpu.roll(w - hi, LANES // 2, axis=1)
        return jnp.dot(packed.astype(BF16), expand, preferred_element_type=F32)

    for sub in range(xb_ref.shape[0] // MERGE_SUB):
        rows = slice(sub * MERGE_SUB, (sub + 1) * MERGE_SUB)
        x = jnp.concatenate([x_ref[j, rows, :] for j in range(N_SLABS)], axis=1)
        xb = xb_ref[rows, :]
        l1, l2, l3 = l1_ref[rows, :], l2_ref[rows, :], l3_ref[rows, :]
        m = jnp.maximum(jnp.maximum(l1, l2), l3)
        e1, e2, e3 = jnp.exp(l1 - m), jnp.exp(l2 - m), jnp.exp(l3 - m)
        inv = 1.0 / (e1 + e2 + e3)
        o3 = o3_ref[rows, :].astype(F32)
        ya = (o3 + per_lane(e1 * inv) * (o1_ref[rows, :].astype(F32) - o3)
              + per_lane(e2 * inv) * (o2_ref[rows, :].astype(F32) - o3))

        def gate(j, xb=xb):
            cols = slice(j * D_MODEL, (j + 1) * D_MODEL)
            return jax.nn.sigmoid(jnp.dot(xb, wg_ref[:, cols], preferred_element_type=F32) + gb_ref[:, cols])

        merged = gate(0) * jnp.dot(ya.astype(BF16), pa_ref[...], preferred_element_type=F32)
        merged += gate(1) * jnp.dot(yb_ref[rows, :], pb_ref[...], preferred_element_type=F32)
        merged += gate(2) * jnp.dot(yc_ref[rows, :], pc_ref[...], preferred_element_type=F32)
        mix = jnp.dot(merged.astype(BF16), wo_ref[...], preferred_element_type=F32)
        out_ref[rows, :] = _layer_norm(ALPHA * x + mix, g_ref[...], b_ref[...])


def merge_ln(x_slabs, xb, o1, o2, o3, l1, l2, l3, yb, yc, w_in, layer, gate_col0, gb, pa, pb, pc, wo, g, b,
             tm=512):
    m = xb.shape[0]
    assert gate_col0 % D_MODEL == 0
    gate_specs = [pl.BlockSpec((None, D_MODEL, D_MODEL), lambda i, cb=gate_col0 // D_MODEL + j: (layer, 0, cb),
                               pipeline_mode=pl.Buffered(1)) for j in range(3)]
    head = jnp.arange(A_W) // HEAD_DIM
    expand = ((jnp.arange(LANES)[:, None] % (LANES // 2)) == head[None, :]).astype(BF16)
    row = lambda w: pl.BlockSpec((tm, w), lambda i: (i, 0))
    return pl.pallas_call(
        _merge_kernel,
        grid=(m // tm,),
        in_specs=[pl.BlockSpec((N_SLABS, tm, LANES), lambda i: (0, i, 0)), row(D_MODEL),
                  row(A_W), row(A_W), row(A_W), row(LANES), row(LANES), row(LANES), row(B_QW), row(C_V),
                  *gate_specs, _full(gb.shape), _layer_of(pa, layer),
                  _layer_of(pb, layer), _layer_of(pc, layer), _layer_of(wo, layer), _full(expand.shape),
                  _full((1, D_MODEL)), _full((1, D_MODEL))],
        out_specs=row(D_MODEL),
        out_shape=jax.ShapeDtypeStruct((m, D_MODEL), F32),
        scratch_shapes=[pltpu.VMEM((D_MODEL, 3 * D_MODEL), BF16)],
        compiler_params=_params(1),
        name="merge_ln",
    )(x_slabs, xb, o1, o2, o3, l1, l2, l3, yb, yc, w_in, w_in, w_in, gb, pa, pb, pc, wo, expand, g, b)


def kernel(x, positions, w_in, gate_bias, attn_sinks, w_proj_a, w_proj_b, w_proj_c, w_out, ffn1_up, ffn1_down, ffn2_up, ffn2_down, ln1_g, ln1_b, ln2_g, ln2_b, ln3_g, ln3_b):
    bsz, seq, _ = x.shape
    m = bsz * seq
    dils = tuple(d for _, d in A_GROUPS)
    tables = {1: rope_tables(positions.reshape(m, 1))}
    tables.update(zip(dils[1:], permute_tables(*tables[1], dils[1:])))
    x = x.reshape(m, D_MODEL)
    a_end = 9 * A_W
    b_end = a_end + B_QW + 2 * B_KVW
    c_end = b_end + 2 * C_QK + 2 * C_V
    qk_scale = HEAD_DIM ** -0.5 * LOG2_E
    seg_a = ((A_W, True, qk_scale), (A_W, True, 1.0))
    seg_b = ((B_QW, True, qk_scale), (B_KVW, True, 1.0))
    seg_c = ((C_QK, True, 1.0), (C_QK, True, C_KEY_DIM ** -0.5), (C_V, False, 1.0), (C_V, False, 1.0))
    row = lambda t: t.reshape(1, -1)
    ffn1 = (ffn1_up.astype(BF16), ffn1_down.astype(BF16))
    ffn2 = (ffn2_up.astype(BF16), ffn2_down.astype(BF16))
    branch_w = tuple(t.astype(BF16) for t in (w_proj_a, w_proj_b, w_proj_c, w_out))
    for l in range(DEPTH):
        x_slabs, xb = ffn_ln(x, *ffn1, l, row(ln1_g[l]), row(ln1_b[l]), True)
        xb_by_dil = dict(zip(dils[1:], permute_tokens(x_slabs, dils[1:])))
        xb_by_dil[1] = xb
        outs, lses = [], []
        for gi, (window, dil) in enumerate(A_GROUPS):
            q, k, vt = proj_rope(xb_by_dil[dil], w_in, l, 3 * gi * A_W, A_W, A_W, *tables[dil], seg_a, "proj_a", tm=PROJ_ROWS_WIDE)
            o, lse = banded_attention(q, k, vt, window // dil, dil, seq)
            outs.append(o)
            lses.append(lse)
        qb, kb, vtb = proj_rope(xb, w_in, l, a_end, MXU_COLS, B_KVW, *tables[1], seg_b, "proj_b", tm=PROJ_ROWS_WIDE)
        yb, = banded_attention(qb, kb, vtb, B_WINDOW - 1, 1, seq, attn_sinks[l], want_lse=False)
        qc, kc, vc, gc = proj_rope(xb, w_in, l, b_end, D_MODEL, 0, *tables[1], seg_c, "proj_c")
        yc = retention_gated(qc.reshape(bsz, seq, C_QK), kc.reshape(bsz, seq, C_QK),
                             vc.reshape(bsz, seq, C_V), gc.reshape(bsz, seq, C_V))
        x = merge_ln(x_slabs, xb, *outs, *lses, yb, yc.reshape(m, C_V),
                     w_in, l, c_end, row(gate_bias[l]), *branch_w, row(ln2_g[l]), row(ln2_b[l]))
        x = ffn_ln(x, *ffn2, l, row(ln3_g[l]), row(ln3_b[l]), False)
    return x.reshape(bsz, seq, D_MODEL)
```

```python
import functools
import math

import jax
import jax.numpy as jnp
from jax import lax
from jax.experimental import pallas as pl
from jax.experimental.pallas import tpu as pltpu

D_MODEL = 1024
DEPTH = 2
HEAD_DIM = 64
BLOCK = 128
A_GROUPS = ((128, 1), (512, 4), (2048, 16))
A_HEADS = 12
A_W = A_HEADS * HEAD_DIM
B_Q_HEADS = 16
B_KV_HEADS = 2
B_WINDOW = 128
B_QW = B_Q_HEADS * HEAD_DIM
B_KVW = B_KV_HEADS * HEAD_DIM
C_HEADS = 8
C_KEY_DIM = 64
C_VAL_DIM = 128
C_CHUNK = 128
C_QK = C_HEADS * C_KEY_DIM
C_V = C_HEADS * C_VAL_DIM
D_FF = 2816
ROPE_THETA = 10000.0
LN_EPS = 1e-5
ALPHA = (2.0 * DEPTH) ** 0.25

LANES = 128
MXU_COLS = 256
FF_CHUNK = 256
FFN_SUB = 256
MERGE_SUB = 256
SPAN = 2048
PLAIN_SPAN = 1024
ATTN_UNITS_BLOCKS = 8
ATTN_LAGS = (3, 4, 7)
RET_LAGS = (1, 2, 3)
NEG_BIG = -1e30
ONES_ROWS = 16
LOG2_E = math.log2(math.e)
LN_2 = math.log(2.0)
VMEM_LIMIT = 56 * 1024 * 1024
N_SLABS = D_MODEL // LANES

F32 = jnp.float32
BF16 = jnp.bfloat16
NT = (((1,), (1,)), ((), ()))


def _params(n_axes):
    return pltpu.CompilerParams(dimension_semantics=("arbitrary",) * n_axes,
                                vmem_limit_bytes=VMEM_LIMIT)


def _layer_norm(y, g, b):
    mu = jnp.mean(y, axis=-1, keepdims=True)
    yc = y - mu
    var = jnp.mean(yc * yc, axis=-1, keepdims=True)
    return yc * lax.rsqrt(var + LN_EPS) * g + b


def _full(shape):
    nd = len(shape)
    return pl.BlockSpec(shape, lambda *_: (0,) * nd, pipeline_mode=pl.Buffered(1))


def _layer_of(stacked, layer):
    nd = stacked.ndim - 1
    return pl.BlockSpec((None,) + stacked.shape[1:], lambda *_: (layer,) + (0,) * nd,
                        pipeline_mode=pl.Buffered(1))


def _software_pipeline(n, stage_a, stage_b, stage_c, stage_d, lags=(1, 2, 3)):
    a_val, b_val, c_val = {}, {}, {}
    lag_b, lag_c, lag_d = lags
    for t in range(n + lag_d):
        if t < n:
            a_val[t] = stage_a(t)
        if 0 <= t - lag_b < n:
            b_val[t - lag_b] = stage_b(t - lag_b, a_val.pop(t - lag_b))
        if 0 <= t - lag_c < n:
            c_val[t - lag_c] = stage_c(t - lag_c, b_val[t - lag_c])
        if 0 <= t - lag_d < n:
            stage_d(t - lag_d, c_val.pop(t - lag_d), b_val.pop(t - lag_d))


def _rope_table_kernel(pos_ref, inv_ref, sign_ref, cos_ref, sin_ref):
    ang = pos_ref[...].astype(F32) * inv_ref[...]
    cos_ref[...] = jnp.cos(ang)
    sin_ref[...] = jnp.sin(ang) * sign_ref[...]


def rope_tables(pos_col, tm=1024):
    m = pos_col.shape[0]
    half = HEAD_DIM // 2
    inv = ROPE_THETA ** (-jnp.arange(half, dtype=F32) / half)
    inv_row = jnp.tile(inv, LANES // half)[None, :]
    lane = jnp.arange(LANES)
    sign_row = jnp.where(lane % HEAD_DIM < half, -1.0, 1.0).astype(F32)[None, :]
    return pl.pallas_call(
        _rope_table_kernel,
        grid=(m // tm,),
        in_specs=[pl.BlockSpec((tm, 1), lambda i: (i, 0)), _full((1, LANES)), _full((1, LANES))],
        out_specs=[pl.BlockSpec((tm, LANES), lambda i: (i, 0))] * 2,
        out_shape=[jax.ShapeDtypeStruct((m, LANES), F32)] * 2,
        compiler_params=_params(1),
        name="rope_table",
    )(pos_col, inv_row, sign_row)


def _ffn_kernel(x_ref, wu_ref, wd_ref, g_ref, b_ref, *out_refs, slab_out):
    n_chunks = D_FF // FF_CHUNK
    for sub in range(x_ref.shape[0] // FFN_SUB):
        rows = slice(sub * FFN_SUB, (sub + 1) * FFN_SUB)
        x = x_ref[rows, :]
        xb = x.astype(BF16)

        def hidden(c, xb=xb):
            lo = c * FF_CHUNK
            a = jnp.dot(xb, wu_ref[:, lo:lo + FF_CHUNK], preferred_element_type=F32)
            b = jnp.dot(xb, wu_ref[:, D_FF + lo:D_FF + lo + FF_CHUNK], preferred_element_type=F32)
            return (a * jax.nn.sigmoid(a) * b).astype(BF16)

        h = hidden(0)
        acc = None
        for c in range(n_chunks):
            h_next = hidden(c + 1) if c + 1 < n_chunks else None
            d = jnp.dot(h, wd_ref[c * FF_CHUNK:(c + 1) * FF_CHUNK, :], preferred_element_type=F32)
            acc = d if acc is None else acc + d
            h = h_next
        y = _layer_norm(ALPHA * x + 0.5 * acc, g_ref[...], b_ref[...])
        if slab_out:
            slab_ref, bf_ref = out_refs
            for j in range(N_SLABS):
                slab_ref[j, rows, :] = y[:, j * LANES:(j + 1) * LANES]
            bf_ref[rows, :] = y.astype(BF16)
        else:
            out_refs[0][rows, :] = y


def ffn_ln(x, wu, wd, layer, g, b, slab_out, tm=1024):
    m = x.shape[0]
    row = pl.BlockSpec((tm, D_MODEL), lambda i: (i, 0))
    if slab_out:
        out_specs = [pl.BlockSpec((N_SLABS, tm, LANES), lambda i: (0, i, 0)), row]
        out_shape = [jax.ShapeDtypeStruct((N_SLABS, m, LANES), F32), jax.ShapeDtypeStruct((m, D_MODEL), BF16)]
    else:
        out_specs = row
        out_shape = jax.ShapeDtypeStruct((m, D_MODEL), F32)
    return pl.pallas_call(
        functools.partial(_ffn_kernel, slab_out=slab_out),
        grid=(m // tm,),
        in_specs=[row, _layer_of(wu, layer), _layer_of(wd, layer), _full((1, D_MODEL)), _full((1, D_MODEL))],
        out_specs=out_specs,
        out_shape=out_shape,
        compiler_params=_params(1),
        name="ffn_ln",
    )(x, wu, wd, g, b)


def _permute_kernel(x_ref, *out_refs, dils):
    for o_ref, d in zip(out_refs, dils):
        chunk = SPAN // d
        for r in range(d):
            for j in range(N_SLABS):
                o_ref[r * chunk:(r + 1) * chunk, j * LANES:(j + 1) * LANES] = (
                    x_ref[j, pl.ds(r, chunk, stride=d), :].astype(BF16))


def permute_tokens(x_slabs, dils):
    m = x_slabs.shape[1]
    return pl.pallas_call(
        functools.partial(_permute_kernel, dils=dils),
        grid=(m // SPAN,),
        in_specs=[pl.BlockSpec((N_SLABS, SPAN, LANES), lambda i: (0, i, 0))],
        out_specs=[pl.BlockSpec((SPAN, D_MODEL), lambda i: (i, 0)) for _ in dils],
        out_shape=[jax.ShapeDtypeStruct((m, D_MODEL), BF16) for _ in dils],
        compiler_params=_params(1),
        name="permute_tokens",
    )(x_slabs)


def _permute_table_kernel(cos_ref, sin_ref, *out_refs, dils):
    outs = iter(out_refs)
    for d in dils:
        chunk = SPAN // d
        for t_ref in (cos_ref, sin_ref):
            o_ref = next(outs)
            for r in range(d):
                o_ref[r * chunk:(r + 1) * chunk, :] = t_ref[pl.ds(r, chunk, stride=d), :]


def permute_tables(cos, sin_signed, dils):
    m = cos.shape[0]
    blk = pl.BlockSpec((SPAN, LANES), lambda i: (i, 0))
    outs = pl.pallas_call(
        functools.partial(_permute_table_kernel, dils=dils),
        grid=(m // SPAN,),
        in_specs=[blk, blk],
        out_specs=[blk] * (2 * len(dils)),
        out_shape=[jax.ShapeDtypeStruct((m, LANES), F32)] * (2 * len(dils)),
        compiler_params=_params(1),
        name="permute_tables",
    )(cos, sin_signed)
    return [(outs[2 * i], outs[2 * i + 1]) for i in range(len(dils))]


def _rope_slab(y, cos, sin_signed, first_half):
    partner = jnp.where(first_half, pltpu.roll(y, LANES - HEAD_DIM // 2, axis=1),
                        pltpu.roll(y, HEAD_DIM // 2, axis=1))
    return y * cos + partner * sin_signed


def _proj_kernel(*refs, n_w, plan, segs, t_width):
    x_ref, w_refs = refs[0], refs[1:1 + n_w]
    cos_ref, sin_ref = refs[1 + n_w:3 + n_w]
    out_refs = refs[3 + n_w:-2]
    w_scr, wt_scr = refs[-2:]

    @pl.when(pl.program_id(0) == 0)
    def _():
        for idx, lo, hi, transposed, dst in plan:
            blk = w_refs[idx][:, lo:hi]
            if transposed:
                wt_scr[dst:dst + hi - lo, :] = blk.T.astype(BF16)
            else:
                w_scr[:, dst:dst + hi - lo] = blk.astype(BF16)

    xb = x_ref[...]
    cos = cos_ref[...]
    sin_signed = sin_ref[...]
    lane = lax.broadcasted_iota(jnp.int32, cos.shape, 1)
    first_half = (lane % HEAD_DIM) < (HEAD_DIM // 2)
    col = 0
    for o_ref, (width, rope, scale) in zip(out_refs, segs):
        for c0 in range(0, width, MXU_COLS):
            cw = min(MXU_COLS, width - c0)
            y2 = jnp.dot(xb, w_scr[:, col + c0:col + c0 + cw], preferred_element_type=F32)
            for s0 in range(0, cw, LANES):
                y = y2[:, s0:s0 + LANES]
                if rope:
                    y = _rope_slab(y, cos, sin_signed, first_half)
                if scale != 1.0:
                    y = y * scale
                o_ref[:, c0 + s0:c0 + s0 + LANES] = y.astype(o_ref.dtype)
        col += width
    if t_width:
        t_ref = out_refs[len(segs)]
        for c0 in range(0, t_width, MXU_COLS):
            cw = min(MXU_COLS, t_width - c0)
            t_ref[c0:c0 + cw, :] = lax.dot_general(wt_scr[c0:c0 + cw, :], xb, NT,
                                                   preferred_element_type=F32).astype(t_ref.dtype)


def proj_rope(xb, w_in, layer, col0, block_w, t_width, cos, sin_signed, segs, name, tm=1024):
    m = xb.shape[0]
    n_row = sum(s[0] for s in segs)
    total = n_row + t_width
    assert col0 % block_w == 0 and total % block_w == 0
    n_w = total // block_w
    plan = []
    for idx in range(n_w):
        lo, hi = idx * block_w, (idx + 1) * block_w
        if lo < n_row:
            plan.append((idx, 0, min(hi, n_row) - lo, False, lo))
        if hi > n_row:
            start = max(lo, n_row)
            plan.append((idx, start - lo, block_w, True, start - n_row))
    w_specs = [pl.BlockSpec((None, D_MODEL, block_w), lambda i, cb=col0 // block_w + idx: (layer, 0, cb),
                            pipeline_mode=pl.Buffered(1)) for idx in range(n_w)]
    row = lambda wd: pl.BlockSpec((tm, wd), lambda i: (i, 0))
    out_specs = [row(s[0]) for s in segs]
    out_shape = [jax.ShapeDtypeStruct((m, s[0]), BF16) for s in segs]
    if t_width:
        out_specs.append(pl.BlockSpec((t_width, tm), lambda i: (0, i)))
        out_shape.append(jax.ShapeDtypeStruct((t_width, m), BF16))
    return pl.pallas_call(
        functools.partial(_proj_kernel, n_w=n_w, plan=tuple(plan), segs=segs, t_width=t_width),
        grid=(m // tm,),
        in_specs=[row(D_MODEL), *w_specs, row(LANES), row(LANES)],
        out_specs=out_specs,
        out_shape=out_shape,
        scratch_shapes=[pltpu.VMEM((D_MODEL, n_row), BF16), pltpu.VMEM((max(t_width, 16), D_MODEL), BF16)],
        compiler_params=_params(1),
        name=name,
    )(xb, *([w_in] * n_w), cos, sin_signed)


def _attn_kernel(*refs, n_heads, gqa, max_dist, dil, n_res, n_blocks, spans_per_batch, has_sink, want_lse):
    refs = list(refs)
    sink_ref = refs.pop(0) if has_sink else None
    n_halo = 1 if n_blocks == 1 else n_res
    q_ref, kc_ref, vc_ref = refs[:3]
    kh_refs, vh_refs = refs[3:3 + n_halo], refs[3 + n_halo:3 + 2 * n_halo]
    o_ref = refs[3 + 2 * n_halo]
    lse_ref = refs[4 + 2 * n_halo] if want_lse else None
    lse_scr, o_scr = refs[-2:]
    g = pl.program_id(0)
    r = pl.program_id(1)
    first_has_prev = (g % spans_per_batch) > 0
    key = lax.broadcasted_iota(jnp.int32, (2 * BLOCK, 2 * BLOCK), 0)
    qry = lax.broadcasted_iota(jnp.int32, (2 * BLOCK, 2 * BLOCK), 1) % BLOCK
    in_band_prev = jnp.logical_and(key < BLOCK, key >= qry + (BLOCK - max_dist))
    in_band_cur = jnp.logical_and(key >= BLOCK, key - BLOCK <= qry)
    bias_on = jnp.where(jnp.logical_or(in_band_prev, in_band_cur), 0.0, NEG_BIG)
    bias_first = jnp.where(jnp.logical_or(jnp.logical_and(in_band_prev, first_has_prev), in_band_cur),
                           0.0, NEG_BIG)
    low = lax.broadcasted_iota(jnp.int32, (BLOCK, LANES), 1) < HEAD_DIM
    lane2 = lax.broadcasted_iota(jnp.int32, (1, 2 * BLOCK), 1)
    ones_rows = jnp.ones((ONES_ROWS, 2 * BLOCK), BF16)
    n_pairs = n_heads // 2
    rep = n_heads // (kc_ref.shape[1] // HEAD_DIM)
    units = [(j, i, p) for j in range(n_res) for i in range(n_blocks) for p in range(n_pairs)]

    def rows_of(j, i):
        return slice((j * n_blocks + i) * BLOCK, (j * n_blocks + i + 1) * BLOCK)

    def prev_and_cur(j, i):
        return slice((j * n_blocks + i - 1) * BLOCK, (j * n_blocks + i + 1) * BLOCK)

    def halo_k(j, kcols):
        return kh_refs[0][j * BLOCK:(j + 1) * BLOCK, kcols] if n_halo == 1 else kh_refs[j][:, kcols]

    def halo_vt(j, vrows):
        return vh_refs[0][vrows, j * BLOCK:(j + 1) * BLOCK] if n_halo == 1 else vh_refs[j][vrows, :]

    def scores(u):
        j, i, p = units[u]
        rows = rows_of(j, i)
        qp = q_ref[rows, p * LANES:(p + 1) * LANES]
        zero = jnp.zeros_like(qp)
        if gqa:
            kv = (2 * p) // rep
            qr = pltpu.roll(qp, HEAD_DIM, axis=1)
            first, second = (qp, qr) if kv == 0 else (qr, qp)
            keep = low if kv == 0 else jnp.logical_not(low)
            rhs = jnp.concatenate([jnp.where(keep, first, zero), jnp.where(keep, second, zero)], axis=0)
            kcols = slice(0, LANES)
        else:
            rhs = jnp.concatenate([jnp.where(low, qp, zero), jnp.where(low, zero, qp)], axis=0)
            kcols = slice(p * LANES, (p + 1) * LANES)
        if i == 0:
            keys = jnp.concatenate([halo_k(j, kcols), kc_ref[rows, kcols]], axis=0)
            bias = bias_first
        else:
            keys, bias = kc_ref[prev_and_cur(j, i), kcols], bias_on
        return lax.dot_general(keys, rhs, NT, preferred_element_type=F32) + bias

    def softmax(u, s):
        p = units[u][2]
        m = jnp.max(s, axis=0, keepdims=True)
        sink_row = None
        if has_sink:
            sink_row = jnp.where(lane2 < BLOCK, sink_ref[2 * p], sink_ref[2 * p + 1]) * LOG2_E
            m = jnp.maximum(m, sink_row)
        return jnp.exp2(s - m).astype(BF16), m, sink_row

    def values(u, sm):
        j, i, p = units[u]
        vrows = (slice(((2 * p) // rep) * HEAD_DIM, ((2 * p) // rep + 1) * HEAD_DIM) if gqa
                 else slice(p * LANES, (p + 1) * LANES))
        if i == 0:
            vt = jnp.concatenate([halo_vt(j, vrows), vc_ref[vrows, rows_of(j, i)]], axis=1)
        else:
            vt = vc_ref[vrows, prev_and_cur(j, i)]
        return jnp.dot(jnp.concatenate([vt, ones_rows], axis=0), sm[0], preferred_element_type=F32)

    def store(u, res, sm):
        j, i, p = units[u]
        _, m, sink_row = sm
        n_feat = res.shape[0] - ONES_ROWS
        den = res[n_feat:n_feat + 1, :]
        if has_sink:
            den = den + jnp.exp2(sink_row - m)
        inv = 1.0 / den
        second = slice(0, HEAD_DIM) if gqa else slice(HEAD_DIM, 2 * HEAD_DIM)
        o_pair = jnp.concatenate([res[:HEAD_DIM, :BLOCK] * inv[:, :BLOCK],
                                  res[second, BLOCK:] * inv[:, BLOCK:]], axis=0)
        if dil == 1:
            dst = rows_of(j, i)
            o_ref[dst, p * LANES:(p + 1) * LANES] = o_pair.T.astype(o_ref.dtype)
        else:
            dst = pl.ds(r * n_res + j + dil * BLOCK * i, BLOCK, stride=dil)
            o_scr[p, dst, :] = o_pair.T
        if want_lse:
            lse = (m + jnp.log2(den)) * LN_2
            if p == 0:
                lse_scr[...] = jnp.zeros_like(lse_scr)
            lse_scr[2 * p:2 * p + 1, :] = lse[:, :BLOCK]
            lse_scr[2 * p + 1:2 * p + 2, :] = lse[:, BLOCK:]
            if p == n_pairs - 1:
                lse_ref[dst, :] = lse_scr[...].T

    _software_pipeline(len(units), scores, softmax, values, store, lags=ATTN_LAGS)

    if dil > 1:
        @pl.when(r == pl.num_programs(1) - 1)
        def _():
            for p in range(n_pairs):
                o_ref[:, p * LANES:(p + 1) * LANES] = o_scr[p].astype(o_ref.dtype)


def banded_attention(q, k, vt, max_dist, dil, seq, sink=None, want_lse=True):
    m, qw = q.shape
    kw = k.shape[1]
    n_heads = qw // HEAD_DIM
    gqa = kw != qw
    span = SPAN if dil > 1 else PLAIN_SPAN
    chunk = span // dil
    n_blocks = chunk // BLOCK
    n_res = max(ATTN_UNITS_BLOCKS // n_blocks, 1)
    spans_per_batch = seq // span
    steps = dil // n_res
    rows = chunk * n_res

    def cur(g, r):
        return (g * steps + r, 0)

    def cur_t(g, r):
        return (0, g * steps + r)

    def halo_block(g, r, j):
        return jnp.maximum(((g - 1) * dil + r * n_res + j + 1) * n_blocks - 1, 0)

    if n_blocks == 1:
        halo = BLOCK * n_res
        k_halos = [pl.BlockSpec((halo, kw), lambda g, r: (halo_block(g, r, 0) // n_res, 0))]
        v_halos = [pl.BlockSpec((kw, halo), lambda g, r: (0, halo_block(g, r, 0) // n_res))]
    else:
        k_halos = [pl.BlockSpec((BLOCK, kw), lambda g, r, j=j: (halo_block(g, r, j), 0)) for j in range(n_res)]
        v_halos = [pl.BlockSpec((kw, BLOCK), lambda g, r, j=j: (0, halo_block(g, r, j))) for j in range(n_res)]
    in_specs = [pl.BlockSpec((rows, qw), cur), pl.BlockSpec((rows, kw), cur), pl.BlockSpec((kw, rows), cur_t),
                *k_halos, *v_halos]
    args = [q, k, vt] + [k] * len(k_halos) + [vt] * len(v_halos)
    if sink is not None:
        in_specs = [pl.BlockSpec(memory_space=pltpu.SMEM)] + in_specs
        args = [sink] + args
    out_specs = [pl.BlockSpec((span, qw), lambda g, r: (g, 0))]
    out_shape = [jax.ShapeDtypeStruct((m, qw), BF16)]
    o_scr_rows = span if dil > 1 else 8
    if want_lse:
        out_specs.append(pl.BlockSpec((span, LANES), lambda g, r: (g, 0)))
        out_shape.append(jax.ShapeDtypeStruct((m, LANES), F32))
    return pl.pallas_call(
        functools.partial(_attn_kernel, n_heads=n_heads, gqa=gqa, max_dist=max_dist, dil=dil, n_res=n_res,
                          n_blocks=n_blocks, spans_per_batch=spans_per_batch,
                          has_sink=sink is not None, want_lse=want_lse),
        grid=(m // span, steps),
        in_specs=in_specs,
        out_specs=out_specs,
        out_shape=out_shape,
        scratch_shapes=[pltpu.VMEM((LANES, BLOCK), F32), pltpu.VMEM((qw // LANES, o_scr_rows, LANES), F32)],
        compiler_params=_params(2),
        name="banded_attention",
    )(*args)


def _retention_kernel(q_ref, k_ref, v_ref, gate_ref, o_ref, state_ref, dec_ref):
    n = pl.program_id(0)
    n_pairs = C_HEADS // 2
    log_g = [math.log1p(-2.0 ** (-5.0 - h)) for h in range(C_HEADS)]
    row = lax.broadcasted_iota(jnp.int32, (C_CHUNK, LANES), 0)
    lane = lax.broadcasted_iota(jnp.int32, (C_CHUNK, LANES), 1)
    low = lane < C_KEY_DIM

    @pl.when(n == 0)
    def _():
        state_ref[...] = jnp.zeros_like(state_ref)
        rel = (row - lane).astype(F32)
        idx = row.astype(F32)
        for h in range(C_HEADS):
            dec_ref[0, h] = jnp.where(rel >= 0, jnp.exp(log_g[h] * jnp.maximum(rel, 0.0)), 0.0)
            dec_ref[1, h] = jnp.exp(log_g[h] * (idx + 1.0))
        for p in range(n_pairs):
            lg = jnp.where(low, log_g[2 * p], log_g[2 * p + 1])
            dec_ref[2, p] = jnp.exp(lg * (C_CHUNK - 1.0 - idx))

    units = [(b, p) for b in range(q_ref.shape[0]) for p in range(n_pairs)]

    def scores(u):
        b, p = units[u]
        cols = slice(p * LANES, (p + 1) * LANES)
        qp, kp = q_ref[b, :, cols], k_ref[b, :, cols]
        zero = jnp.zeros_like(qp)
        lhs = jnp.concatenate([jnp.where(low, qp, zero), jnp.where(low, zero, qp)], axis=0)
        s = lax.dot_general(lhs, kp, NT, preferred_element_type=F32)
        qs = jnp.dot(lhs, state_ref[b, p].astype(BF16), preferred_element_type=F32)
        return s, qs, kp

    def decay(u, sc):
        _, p = units[u]
        s, qs, kp = sc
        a = [(s[hh * C_CHUNK:(hh + 1) * C_CHUNK] * dec_ref[0, 2 * p + hh]).astype(BF16) for hh in range(2)]
        kd_t = (kp.astype(F32) * dec_ref[2, p]).T.astype(BF16)
        return a, kd_t, qs

    def values(u, dc):
        b, p = units[u]
        a, kd_t, qs = dc
        outs = []
        for hh in range(2):
            h = 2 * p + hh
            v = v_ref[b, :, h * C_VAL_DIM:(h + 1) * C_VAL_DIM]
            srows = slice(hh * C_KEY_DIM, (hh + 1) * C_KEY_DIM)
            res = jnp.dot(jnp.concatenate([a[hh], kd_t[srows, :]], axis=0), v, preferred_element_type=F32)
            outs.append(res[:C_CHUNK] + qs[hh * C_CHUNK:(hh + 1) * C_CHUNK] * dec_ref[1, h])
            state_ref[b, p, srows, :] = (state_ref[b, p, srows, :] * math.exp(log_g[h] * C_CHUNK)
                                         + res[C_CHUNK:])
        return outs

    def store(u, outs):
        b, p = units[u]
        for hh in range(2):
            h = 2 * p + hh
            vs = slice(h * C_VAL_DIM, (h + 1) * C_VAL_DIM)
            o = outs[hh]
            mu = jnp.mean(o, axis=-1, keepdims=True)
            oc = o - mu
            var = jnp.mean(oc * oc, axis=-1, keepdims=True)
            gate = gate_ref[b, :, vs].astype(F32)
            o_ref[b, :, vs] = (gate * jax.nn.sigmoid(gate) * (oc * lax.rsqrt(var + LN_EPS))).astype(o_ref.dtype)

    _software_pipeline(len(units), scores, decay, values, lambda u, outs, _: store(u, outs), lags=RET_LAGS)


def retention_gated(q, k, v, gate):
    bsz, seq, _ = q.shape
    blk = lambda w: pl.BlockSpec((bsz, C_CHUNK, w), lambda n: (0, n, 0))
    return pl.pallas_call(
        _retention_kernel,
        grid=(seq // C_CHUNK,),
        in_specs=[blk(C_QK), blk(C_QK), blk(C_V), blk(C_V)],
        out_specs=blk(C_V),
        out_shape=jax.ShapeDtypeStruct((bsz, seq, C_V), BF16),
        scratch_shapes=[pltpu.VMEM((bsz, C_HEADS // 2, 2 * C_KEY_DIM, C_VAL_DIM), F32),
                        pltpu.VMEM((3, C_HEADS, C_CHUNK, LANES), F32)],
        compiler_params=_params(1),
        name="retention",
    )(q, k, v, gate)


def _merge_kernel(x_ref, xb_ref, o1_ref, o2_ref, o3_ref, l1_ref, l2_ref, l3_ref, yb_ref, yc_ref,
                  wg0_ref, wg1_ref, wg2_ref, gb_ref, pa_ref, pb_ref, pc_ref, wo_ref, e_ref, g_ref, b_ref,
                  out_ref, wg_ref):
    @pl.when(pl.program_id(0) == 0)
    def _():
        for j, blk_ref in enumerate((wg0_ref, wg1_ref, wg2_ref)):
            wg_ref[:, j * D_MODEL:(j + 1) * D_MODEL] = blk_ref[...].astype(BF16)

    expand = e_ref[...]
    lane = lax.broadcasted_iota(jnp.int32, (MERGE_SUB, LANES), 1)

    def per_lane(w):
        w = jnp.where(lane < A_HEADS, w, 0.0)
        hi = w.astype(BF16).astype(F32)
        packed = hi + pltpu.roll(w - hi, LANES // 2, axis=1)
        return jnp.dot(packed.astype(BF16), expand, preferred_element_type=F32)

    for sub in range(xb_ref.shape[0] // MERGE_SUB):
        rows = slice(sub * MERGE_SUB, (sub + 1) * MERGE_SUB)
        x = jnp.concatenate([x_ref[j, rows, :] for j in range(N_SLABS)], axis=1)
        xb = xb_ref[rows, :]
        l1, l2, l3 = l1_ref[rows, :], l2_ref[rows, :], l3_ref[rows, :]
        m = jnp.maximum(jnp.maximum(l1, l2), l3)
        e1, e2, e3 = jnp.exp(l1 - m), jnp.exp(l2 - m), jnp.exp(l3 - m)
        inv = 1.0 / (e1 + e2 + e3)
        o3 = o3_ref[rows, :].astype(F32)
        ya = (o3 + per_lane(e1 * inv) * (o1_ref[rows, :].astype(F32) - o3)
              + per_lane(e2 * inv) * (o2_ref[rows, :].astype(F32) - o3))

        def gate(j, xb=xb):
            cols = slice(j * D_MODEL, (j + 1) * D_MODEL)
            return jax.nn.sigmoid(jnp.dot(xb, wg_ref[:, cols], preferred_element_type=F32) + gb_ref[:, cols])

        merged = gate(0) * jnp.dot(ya.astype(BF16), pa_ref[...], preferred_element_type=F32)
        merged += gate(1) * jnp.dot(yb_ref[rows, :], pb_ref[...], preferred_element_type=F32)
        merged += gate(2) * jnp.dot(yc_ref[rows, :], pc_ref[...], preferred_element_type=F32)
        mix = jnp.dot(merged.astype(BF16), wo_ref[...], preferred_element_type=F32)
        out_ref[rows, :] = _layer_norm(ALPHA * x + mix, g_ref[...], b_ref[...])


def merge_ln(x_slabs, xb, o1, o2, o3, l1, l2, l3, yb, yc, w_in, layer, gate_col0, gb, pa, pb, pc, wo, g, b,
             tm=512):
    m = xb.shape[0]
    assert gate_col0 % D_MODEL == 0
    gate_specs = [pl.BlockSpec((None, D_MODEL, D_MODEL), lambda i, cb=gate_col0 // D_MODEL + j: (layer, 0, cb),
                               pipeline_mode=pl.Buffered(1)) for j in range(3)]
    head = jnp.arange(A_W) // HEAD_DIM
    expand = ((jnp.arange(LANES)[:, None] % (LANES // 2)) == head[None, :]).astype(BF16)
    row = lambda w: pl.BlockSpec((tm, w), lambda i: (i, 0))
    return pl.pallas_call(
        _merge_kernel,
        grid=(m // tm,),
        in_specs=[pl.BlockSpec((N_SLABS, tm, LANES), lambda i: (0, i, 0)), row(D_MODEL),
                  row(A_W), row(A_W), row(A_W), row(LANES), row(LANES), row(LANES), row(B_QW), row(C_V),
                  *gate_specs, _full(gb.shape), _layer_of(pa, layer),
                  _layer_of(pb, layer), _layer_of(pc, layer), _layer_of(wo, layer), _full(expand.shape),
                  _full((1, D_MODEL)), _full((1, D_MODEL))],
        out_specs=row(D_MODEL),
        out_shape=jax.ShapeDtypeStruct((m, D_MODEL), F32),
        scratch_shapes=[pltpu.VMEM((D_MODEL, 3 * D_MODEL), BF16)],
        compiler_params=_params(1),
        name="merge_ln",
    )(x_slabs, xb, o1, o2, o3, l1, l2, l3, yb, yc, w_in, w_in, w_in, gb, pa, pb, pc, wo, expand, g, b)


def kernel(x, positions, w_in, gate_bias, attn_sinks, w_proj_a, w_proj_b, w_proj_c, w_out, ffn1_up, ffn1_down, ffn2_up, ffn2_down, ln1_g, ln1_b, ln2_g, ln2_b, ln3_g, ln3_b):
    bsz, seq, _ = x.shape
    m = bsz * seq
    dils = tuple(d for _, d in A_GROUPS)
    tables = {1: rope_tables(positions.reshape(m, 1))}
    tables.update(zip(dils[1:], permute_tables(*tables[1], dils[1:])))
    x = x.reshape(m, D_MODEL)
    a_end = 9 * A_W
    b_end = a_end + B_QW + 2 * B_KVW
    c_end = b_end + 2 * C_QK + 2 * C_V
    qk_scale = HEAD_DIM ** -0.5 * LOG2_E
    seg_a = ((A_W, True, qk_scale), (A_W, True, 1.0))
    seg_b = ((B_QW, True, qk_scale), (B_KVW, True, 1.0))
    seg_c = ((C_QK, True, 1.0), (C_QK, True, C_KEY_DIM ** -0.5), (C_V, False, 1.0), (C_V, False, 1.0))
    row = lambda t: t.reshape(1, -1)
    ffn1 = (ffn1_up.astype(BF16), ffn1_down.astype(BF16))
    ffn2 = (ffn2_up.astype(BF16), ffn2_down.astype(BF16))
    branch_w = tuple(t.astype(BF16) for t in (w_proj_a, w_proj_b, w_proj_c, w_out))
    for l in range(DEPTH):
        x_slabs, xb = ffn_ln(x, *ffn1, l, row(ln1_g[l]), row(ln1_b[l]), True)
        xb_by_dil = dict(zip(dils[1:], permute_tokens(x_slabs, dils[1:])))
        xb_by_dil[1] = xb
        outs, lses = [], []
        for gi, (window, dil) in enumerate(A_GROUPS):
            q, k, vt = proj_rope(xb_by_dil[dil], w_in, l, 3 * gi * A_W, A_W, A_W, *tables[dil], seg_a, "proj_a")
            o, lse = banded_attention(q, k, vt, window // dil, dil, seq)
            outs.append(o)
            lses.append(lse)
        qb, kb, vtb = proj_rope(xb, w_in, l, a_end, MXU_COLS, B_KVW, *tables[1], seg_b, "proj_b")
        yb, = banded_attention(qb, kb, vtb, B_WINDOW - 1, 1, seq, attn_sinks[l], want_lse=False)
        qc, kc, vc, gc = proj_rope(xb, w_in, l, b_end, D_MODEL, 0, *tables[1], seg_c, "proj_c")
        yc = retention_gated(qc.reshape(bsz, seq, C_QK), kc.reshape(bsz, seq, C_QK),
                             vc.reshape(bsz, seq, C_V), gc.reshape(bsz, seq, C_V))
        x = merge_ln(x_slabs, xb, *outs, *lses, yb, yc.reshape(m, C_V),
                     w_in, l, c_end, row(gate_bias[l]), *branch_w, row(ln2_g[l]), row(ln2_b[l]))
        x = ffn_ln(x, *ffn2, l, row(ln3_g[l]), row(ln3_b[l]), False)
    return x.reshape(bsz, seq, D_MODEL)
```
